```python
import jax
import jax.numpy as jnp
from jax import lax
import numpy as np

D_MODEL = 1024
BATCH = 32
SEQ = 2048
DEPTH = 2
DEC_BATCH = 128
DEC_SEQ = 8
PAST_LEN = 16384
PAGE_SIZE = 128

HEAD_DIM = 64
ROPE_THETA = 10000.0
NORM_EPS = 1e-6
QUERY_BLOCK = 128

A_HEADS = 4
IDX_HEADS = 4
IDX_DIM = 64
DSA_TOPK = 256

B_HEADS = 8
Q_LORA = 256
KV_LORA = 128
NOPE_DIM = 64
ROPE_DIM_B = 32
V_DIM_B = 64
MLA_SCALE = (NOPE_DIM + ROPE_DIM_B) ** -0.5

C_HEADS = 4
C_KV_HEADS = 2
MOBA_BLOCK = 256
MOBA_TOPK = 3
MOBA_Q_CHUNK = 16

WIDTH_A = A_HEADS * HEAD_DIM
WIDTH_B = B_HEADS * V_DIM_B
WIDTH_C = C_HEADS * HEAD_DIM
MIX_WIDTH = WIDTH_A + WIDTH_B + WIDTH_C

IN_SPLITS = (
    ('a_q', A_HEADS * HEAD_DIM), ('a_k', HEAD_DIM), ('a_v', HEAD_DIM),
    ('a_qi', IDX_HEADS * IDX_DIM), ('a_ki', IDX_DIM), ('a_w', IDX_HEADS), ('a_gate', WIDTH_A),
    ('b_cq', Q_LORA), ('b_ckv', KV_LORA), ('b_kr', ROPE_DIM_B), ('b_gate', WIDTH_B),
    ('c_q', C_HEADS * HEAD_DIM), ('c_k', C_KV_HEADS * HEAD_DIM), ('c_v', C_KV_HEADS * HEAD_DIM), ('c_gate', WIDTH_C),
)
IN_WIDTH = sum(w for _, w in IN_SPLITS)

kernel_name = "hybrid_dsa_mla_moba_step"


def rms_norm(x, g):
    xf = x.astype(jnp.float32)
    y = xf * lax.rsqrt(jnp.mean(xf * xf, axis=-1, keepdims=True) + NORM_EPS)
    return (y * g.astype(jnp.float32)).astype(x.dtype)


def rope(x, pos):
    half = x.shape[-1] // 2
    inv_freq = ROPE_THETA ** (-jnp.arange(half, dtype=jnp.float32) / half)
    ang = pos.astype(jnp.float32)[:, None] * inv_freq[None, :]
    cos = jnp.cos(ang)[None, :, None, :]
    sin = jnp.sin(ang)[None, :, None, :]
    xf = x.astype(jnp.float32)
    x1, x2 = xf[..., :half], xf[..., half:]
    return jnp.concatenate([x1 * cos - x2 * sin, x2 * cos + x1 * sin], axis=-1).astype(x.dtype)


def masked_softmax(s, mask):
    return jax.nn.softmax(jnp.where(mask, s.astype(jnp.float32), -jnp.inf), axis=-1)


def split_columns(z):
    out = {}
    off = 0
    for name, w in IN_SPLITS:
        out[name] = z[..., off:off + w]
        off += w
    return out


def map_query_blocks(fn, q_arrays, q_pos, blk):
    T = q_pos.shape[0]
    blk = min(blk, T)
    n = T // blk

    def split(a):
        return jnp.moveaxis(a.reshape((a.shape[0], n, blk) + a.shape[2:]), 1, 0)

    out = lax.map(lambda args: fn(*args), tuple(split(a) for a in q_arrays) + (q_pos.reshape(n, blk),))
    out = jnp.moveaxis(out, 0, 1)
    return out.reshape((out.shape[0], T) + out.shape[3:])


def gather_past(pool, layer, page_table):
    g = pool[layer, page_table]
    return g.reshape((g.shape[0], g.shape[1] * g.shape[2]) + g.shape[3:])


def paged_rows(pool, layer, page_table, new_rows, pos):
    page = pool.shape[2]
    past = page_table.shape[1] * page
    bi = jnp.arange(pos.shape[0]).reshape((-1,) + (1,) * (pos.ndim - 1))
    pp = jnp.minimum(pos, past - 1)
    phys = page_table[bi, pp // page]
    old = pool[layer, phys, pp % page]
    new = new_rows[bi, jnp.clip(pos - past, 0, new_rows.shape[1] - 1)]
    is_past = (pos < past).reshape(pos.shape + (1,) * (new_rows.ndim - 2))
    return jnp.where(is_past, old, new)


def dsa_attend(q, qi, w, ki_all, q_pos, fetch_kv):
    L = ki_all.shape[1]
    k_top = min(DSA_TOPK, L // 4)
    rel = jax.nn.relu(jnp.einsum('bthd,bsd->bths', qi, ki_all).astype(jnp.float32) * IDX_DIM ** -0.5)
    score = jnp.einsum('bths,bth->bts', rel, w.astype(jnp.float32) * IDX_HEADS ** -0.5)
    causal = jnp.arange(L)[None, :] <= q_pos[:, None]
    score = jnp.where(causal[None], score, -jnp.inf)
    _, sel = lax.top_k(score, k_top)
    valid = sel <= q_pos[None, :, None]
    k_sel, v_sel = fetch_kv(sel)
    s = jnp.einsum('bthd,btkd->bthk', q, k_sel) * HEAD_DIM ** -0.5
    p = masked_softmax(s, valid[:, :, None, :]).astype(v_sel.dtype)
    return jnp.einsum('bthk,btkd->bthd', p, v_sel)


def mla_attend(q_lat, q_rope, kv_lat, k_rope, q_pos):
    L = kv_lat.shape[1]
    s = (jnp.einsum('bthc,bsc->bhts', q_lat, kv_lat) + jnp.einsum('bthr,bsr->bhts', q_rope, k_rope)) * MLA_SCALE
    causal = jnp.arange(L)[None, :] <= q_pos[:, None]
    p = masked_softmax(s, causal).astype(kv_lat.dtype)
    return jnp.einsum('bhts,bsc->bthc', p, kv_lat)


def moba_blocks(k, v):
    B, L = k.shape[:2]
    nb = -(-L // MOBA_BLOCK)
    widths = ((0, 0), (0, nb * MOBA_BLOCK - L), (0, 0), (0, 0))
    kb = jnp.pad(k, widths).reshape(B, nb, MOBA_BLOCK, C_KV_HEADS, HEAD_DIM)
    vb = jnp.pad(v, widths).reshape(B, nb, MOBA_BLOCK, C_KV_HEADS, HEAD_DIM)
    kmean = jnp.mean(kb.astype(jnp.float32), axis=2)
    return kb, vb, kmean


def moba_attend(q, q_pos, kb, vb, kmean):
    B, T = q.shape[:2]
    nb = kb.shape[1]
    group = C_HEADS // C_KV_HEADS
    own = q_pos // MOBA_BLOCK
    own_idx = jnp.broadcast_to(own[None, :, None, None], (B, T, C_HEADS, 1))
    n_sel = min(MOBA_TOPK, nb - 1)
    if n_sel > 0:
        qg = q.astype(jnp.float32).reshape(B, T, C_KV_HEADS, group, HEAD_DIM)
        gate = jnp.einsum('btkgd,bnkd->btkgn', qg, kmean).reshape(B, T, C_HEADS, nb)
        past_block = jnp.arange(nb)[None, None, None, :] < own[None, :, None, None]
        _, top = lax.top_k(jnp.where(past_block, gate, -jnp.inf), n_sel)
        idx = jnp.concatenate([top.astype(own_idx.dtype), own_idx], axis=-1)
    else:
        idx = own_idx
    n_slot = idx.shape[-1]
    bi = jnp.arange(B)[:, None, None, None]
    hi = (jnp.arange(C_HEADS) // group)[None, None, :, None]
    k_sel = kb[bi, idx, :, hi, :]
    v_sel = vb[bi, idx, :, hi, :]
    key_pos = idx[..., None] * MOBA_BLOCK + jnp.arange(MOBA_BLOCK)
    is_own = (jnp.arange(n_slot) == n_slot - 1)[None, None, None, :]
    block_ok = is_own | (idx < own[None, :, None, None])
    mask = block_ok[..., None] & (key_pos <= q_pos[None, :, None, None, None])
    s = jnp.einsum('bthd,bthsnd->bthsn', q, k_sel) * HEAD_DIM ** -0.5
    flat = (B, T, C_HEADS, n_slot * MOBA_BLOCK)
    p = masked_softmax(s.reshape(flat), mask.reshape(flat))
    p = p.reshape(B, T, C_HEADS, n_slot, MOBA_BLOCK).astype(v_sel.dtype)
    return jnp.einsum('bthsn,bthsnd->bthd', p, v_sel)


def project(x, pos, layer, wts):
    norm_g, w_in, w_out, mla_g_q, mla_w_uq, mla_g_kv, mla_w_uk, mla_w_uv = wts
    B, T = x.shape[:2]
    h = rms_norm(x, norm_g[layer])
    cols = split_columns(jnp.einsum('btd,dp->btp', h, w_in[layer]))
    t = {}
    t['a_q'] = rope(cols['a_q'].reshape(B, T, A_HEADS, HEAD_DIM), pos)
    t['a_k'] = rope(cols['a_k'].reshape(B, T, 1, HEAD_DIM), pos)[:, :, 0]
    t['a_v'] = cols['a_v']
    t['a_qi'] = rope(cols['a_qi'].reshape(B, T, IDX_HEADS, IDX_DIM), pos)
    t['a_ki'] = rope(cols['a_ki'].reshape(B, T, 1, IDX_DIM), pos)[:, :, 0]
    t['a_w'] = cols['a_w']
    cq = rms_norm(cols['b_cq'], mla_g_q[layer])
    qb = jnp.einsum('btc,chn->bthn', cq, mla_w_uq[layer])
    t['q_rope'] = rope(qb[..., NOPE_DIM:], pos)
    t['q_lat'] = jnp.einsum('bthn,chn->bthc', qb[..., :NOPE_DIM], mla_w_uk[layer])
    t['ckv'] = rms_norm(cols['b_ckv'], mla_g_kv[layer])
    t['kr'] = rope(cols['b_kr'].reshape(B, T, 1, ROPE_DIM_B), pos)[:, :, 0]
    t['c_q'] = rope(cols['c_q'].reshape(B, T, C_HEADS, HEAD_DIM), pos)
    t['c_k'] = rope(cols['c_k'].reshape(B, T, C_KV_HEADS, HEAD_DIM), pos)
    t['c_v'] = cols['c_v'].reshape(B, T, C_KV_HEADS, HEAD_DIM)
    t['a_gate'] = cols['a_gate']
    t['b_gate'] = cols['b_gate']
    t['c_gate'] = cols['c_gate']
    return t


def cache_rows(t):
    dsa_kv = jnp.stack([t['a_k'], t['a_v']], axis=2)
    dsa_idx = t['a_ki']
    mla = jnp.concatenate([t['ckv'], t['kr']], axis=-1)
    moba_kv = jnp.stack([t['c_k'], t['c_v']], axis=2)
    return (dsa_kv, dsa_idx, mla, moba_kv)


def mix_output(x, layer, t, a_o, b_lat, c_o, wts):
    norm_g, w_in, w_out, mla_g_q, mla_w_uq, mla_g_kv, mla_w_uk, mla_w_uv = wts
    B, T = x.shape[:2]
    b_o = jnp.einsum('bthc,chn->bthn', b_lat, mla_w_uv[layer])
    mix = jnp.concatenate([
        a_o.reshape(B, T, WIDTH_A) * jax.nn.silu(t['a_gate']),
        b_o.reshape(B, T, WIDTH_B) * jax.nn.silu(t['b_gate']),
        c_o.reshape(B, T, WIDTH_C) * jax.nn.silu(t['c_gate']),
    ], axis=-1)
    return x + jnp.einsum('btm,md->btd', mix, w_out[layer])


def prompt_layer(x, pos, layer, wts):
    t = project(x, pos, layer, wts)
    B = x.shape[0]
    bi = jnp.arange(B)[:, None, None]
    a_k, a_v, a_ki = t['a_k'], t['a_v'], t['a_ki']

    def dsa_block(qb, qib, wb, pb):
        return dsa_attend(qb, qib, wb, a_ki, pb, lambda sel: (a_k[bi, sel], a_v[bi, sel]))

    a_o = map_query_blocks(dsa_block, (t['a_q'], t['a_qi'], t['a_w']), pos, QUERY_BLOCK)
    ckv, kr = t['ckv'], t['kr']
    b_lat = map_query_blocks(lambda ql, qr, pb: mla_attend(ql, qr, ckv, kr, pb),
                             (t['q_lat'], t['q_rope']), pos, QUERY_BLOCK)
    kb, vb, kmean = moba_blocks(t['c_k'], t['c_v'])
    c_o = map_query_blocks(lambda qb, pb: moba_attend(qb, pb, kb, vb, kmean), (t['c_q'],), pos, MOBA_Q_CHUNK)
    return mix_output(x, layer, t, a_o, b_lat, c_o, wts), cache_rows(t)


def sample_layer(x, pos, layer, wts, cache_dsa_kv, cache_dsa_idx, cache_mla, cache_moba_kv, page_table):
    t = project(x, pos, layer, wts)
    rows = cache_rows(t)
    dsa_kv_row, dsa_idx_row, mla_row, moba_row = rows
    ki_all = jnp.concatenate([gather_past(cache_dsa_idx, layer, page_table), dsa_idx_row], axis=1)

    def fetch(sel):
        r = paged_rows(cache_dsa_kv, layer, page_table, dsa_kv_row, sel)
        return r[..., 0, :], r[..., 1, :]

    a_o = dsa_attend(t['a_q'], t['a_qi'], t['a_w'], ki_all, pos, fetch)
    lat = jnp.concatenate([gather_past(cache_mla, layer, page_table), mla_row], axis=1)
    b_lat = mla_attend(t['q_lat'], t['q_rope'], lat[..., :KV_LORA], lat[..., KV_LORA:], pos)
    kv = jnp.concatenate([gather_past(cache_moba_kv, layer, page_table), moba_row], axis=1)
    kb, vb, kmean = moba_blocks(kv[:, :, 0], kv[:, :, 1])
    c_o = moba_attend(t['c_q'], pos, kb, vb, kmean)
    return mix_output(x, layer, t, a_o, b_lat, c_o, wts), rows


def setup_inputs(seed: int = 0) -> dict:
    key = jax.random.key(seed)
    ks = jax.random.split(key, 16)
    f32 = jnp.float32
    n_pages = PAST_LEN // PAGE_SIZE
    n_used = DEC_BATCH * n_pages
    n_phys = n_used + n_used // 4
    nrm = jax.random.normal
    return {
        'x_prompt': nrm(ks[0], (BATCH, SEQ, D_MODEL), f32),
        'x_sample': nrm(ks[1], (DEC_BATCH, DEC_SEQ, D_MODEL), f32),
        'cache_dsa_kv': nrm(ks[2], (DEPTH, n_phys, PAGE_SIZE, 2, HEAD_DIM), f32),
        'cache_dsa_idx': nrm(ks[3], (DEPTH, n_phys, PAGE_SIZE, IDX_DIM), f32),
        'cache_mla': nrm(ks[4], (DEPTH, n_phys, PAGE_SIZE, KV_LORA + ROPE_DIM_B), f32),
        'cache_moba_kv': nrm(ks[5], (DEPTH, n_phys, PAGE_SIZE, 2, C_KV_HEADS, HEAD_DIM), f32),
        'page_table': jax.random.permutation(ks[6], n_phys)[:n_used].reshape(DEC_BATCH, n_pages).astype(jnp.int32),
        'norm_g': 1.0 + 0.02 * nrm(ks[7], (DEPTH, D_MODEL), f32),
        'w_in': nrm(ks[8], (DEPTH, D_MODEL, IN_WIDTH), f32) * D_MODEL ** -0.5,
        'w_out': nrm(ks[9], (DEPTH, MIX_WIDTH, D_MODEL), f32) * MIX_WIDTH ** -0.5,
        'mla_g_q': 1.0 + 0.02 * nrm(ks[10], (DEPTH, Q_LORA), f32),
        'mla_w_uq': nrm(ks[11], (DEPTH, Q_LORA, B_HEADS, NOPE_DIM + ROPE_DIM_B), f32) * Q_LORA ** -0.5,
        'mla_g_kv': 1.0 + 0.02 * nrm(ks[12], (DEPTH, KV_LORA), f32),
        'mla_w_uk': nrm(ks[13], (DEPTH, KV_LORA, B_HEADS, NOPE_DIM), f32) * KV_LORA ** -0.5,
        'mla_w_uv': nrm(ks[14], (DEPTH, KV_LORA, B_HEADS, V_DIM_B), f32) * KV_LORA ** -0.5,
        'final_g': 1.0 + 0.02 * nrm(ks[15], (D_MODEL,), f32),
    }


def reference(x_prompt, x_sample, cache_dsa_kv, cache_dsa_idx, cache_mla, cache_moba_kv, page_table,
              norm_g, w_in, w_out, mla_g_q, mla_w_uq, mla_g_kv, mla_w_uk, mla_w_uv, final_g):
    wts = (norm_g, w_in, w_out, mla_g_q, mla_w_uq, mla_g_kv, mla_w_uk, mla_w_uv)
    past = page_table.shape[1] * cache_mla.shape[2]
    pos_p = jnp.arange(x_prompt.shape[1], dtype=jnp.int32)
    pos_s = past + jnp.arange(x_sample.shape[1], dtype=jnp.int32)
    xp, xs = x_prompt, x_sample
    rows_p, rows_s = [], []
    for layer in range(DEPTH):
        xp, rp = prompt_layer(xp, pos_p, layer, wts)
        xs, rs = sample_layer(xs, pos_s, layer, wts, cache_dsa_kv, cache_dsa_idx, cache_mla, cache_moba_kv, page_table)
        rows_p.append(rp)
        rows_s.append(rs)
    y_prompt = rms_norm(xp, final_g)
    y_sample = rms_norm(xs, final_g)
    return (y_prompt, y_sample,
            jnp.stack([r[0] for r in rows_p]), jnp.stack([r[1] for r in rows_p]),
            jnp.stack([r[2] for r in rows_p]), jnp.stack([r[3] for r in rows_p]),
            jnp.stack([r[0] for r in rows_s]), jnp.stack([r[1] for r in rows_s]),
            jnp.stack([r[2] for r in rows_s]), jnp.stack([r[3] for r in rows_s]))
```

```python
import functools

import jax
import jax.numpy as jnp
from jax import lax
from jax.experimental import pallas as pl
from jax.experimental.pallas import tpu as pltpu

F32 = jnp.float32
BF16 = jnp.bfloat16
NEG_INF = float("-inf")
HIGHEST = lax.Precision.HIGHEST

HEAD_DIM = 64
ROPE_THETA = 10000.0
NORM_EPS = 1e-6
A_HEADS = 4
IDX_HEADS = 4
IDX_DIM = 64
DSA_TOPK = 256
B_HEADS = 8
Q_LORA = 256
KV_LORA = 128
NOPE_DIM = 64
ROPE_DIM_B = 32
V_DIM_B = 64
MLA_SCALE = (NOPE_DIM + ROPE_DIM_B) ** -0.5
C_HEADS = 4
C_KV_HEADS = 2
MOBA_BLOCK = 256
MOBA_TOPK = 3
LAT_DIM = KV_LORA + ROPE_DIM_B

LANES = 128
VMEM_LIMIT_BYTES = 56 * 1024 * 1024

_SPLITS = (
    ("a_q", 256), ("a_k", 64), ("a_v", 64), ("a_qi", 256), ("a_ki", 64), ("a_w", 4), ("a_gate", 256),
    ("b_cq", 256), ("b_ckv", 128), ("b_kr", 32), ("b_gate", 512),
    ("c_q", 256), ("c_k", 128), ("c_v", 128), ("c_gate", 256),
)
_OFF = {}
_o = 0
for _n, _w in _SPLITS:
    _OFF[_n] = (_o, _o + _w)
    _o += _w
IN_WIDTH = _o
_PERM = ("a_q", "a_qi", "c_q", "c_k", "a_k", "a_ki", "a_gate", "b_gate", "c_gate", "b_cq", "b_ckv", "c_v",
         "a_v", "b_kr", "a_w")
PROJ_WIDTH = 2688
TAB_WIDTH = 6 * LANES


def _dot(a, b, precision=None):
    return jnp.dot(a, b, preferred_element_type=F32, precision=precision)


def _dot_nt(a, b):
    return lax.dot_general(a, b, (((1,), (1,)), ((), ())), preferred_element_type=F32)


def _lane_iota(shape):
    return lax.broadcasted_iota(jnp.int32, shape, len(shape) - 1)


def _row_iota(shape):
    return lax.broadcasted_iota(jnp.int32, shape, len(shape) - 2)


def _swap_half(x, half):
    n = x.shape[-1]
    lane = _lane_iota(x.shape)
    fwd = pltpu.roll(x, n - half, 1)
    bwd = pltpu.roll(x, half, 1)
    return jnp.where((lane & (2 * half - 1)) < half, fwd, bwd)


def _rms(x, g):
    return x * lax.rsqrt(jnp.mean(x * x, axis=-1, keepdims=True) + NORM_EPS) * g


def _topk_mask(score, k):
    rows, width = score.shape
    bits = lax.bitcast_convert_type(score, jnp.int32)
    u = jnp.where(bits < 0, bits ^ jnp.int32(0x7FFFFFFF), bits)
    kf = float(k)

    def count_ge(t):
        return jnp.sum(jnp.where(u >= t, 1.0, 0.0), axis=-1, keepdims=True)

    zero = jnp.zeros((rows, 1), jnp.int32)
    base = jnp.where(count_ge(zero) >= kf, zero, jnp.int32(-2 ** 31))

    def value_step(it, base):
        cand = base | jnp.left_shift(jnp.int32(1), 30 - it)
        return jnp.where(count_ge(cand) >= kf, cand, base)

    thr = lax.fori_loop(0, 31, value_step, base)
    gt = u > thr
    eq = u == thr
    need = kf - jnp.sum(jnp.where(gt, 1.0, 0.0), axis=-1, keepdims=True)
    idx = _lane_iota(score.shape)
    nbits = int(width).bit_length()

    def index_step(it, lim):
        cand = lim | jnp.left_shift(jnp.int32(1), nbits - 1 - it)
        below = jnp.sum(jnp.where(eq & (idx < cand), 1.0, 0.0), axis=-1, keepdims=True)
        return jnp.where(below < need, cand, lim)

    lim = lax.fori_loop(0, nbits, index_step, zero)
    return gt | (eq & (idx <= lim))


def _topn_lanes(gate, allowed, n_sel):
    lanef = _lane_iota(gate.shape).astype(F32)
    gm = jnp.where(allowed, gate, NEG_INF)
    sel = jnp.zeros(gate.shape, jnp.bool_)
    for _ in range(n_sel):
        m = jnp.max(gm, axis=-1, keepdims=True)
        first = jnp.min(jnp.where(gm == m, lanef, float(LANES)), axis=-1, keepdims=True)
        pick = lanef == first
        sel = sel | pick
        gm = jnp.where(pick, NEG_INF, gm)
    return sel & allowed


def _softmax_pv(s, allowed, v_t):
    s = jnp.where(allowed, s, NEG_INF)
    m = jnp.max(s, axis=-1, keepdims=True)
    p = jnp.exp(s - m)
    l = jnp.sum(p, axis=-1, keepdims=True)
    return _dot_nt(p.astype(BF16), v_t) / l


def _flash_update(s, v_t, m_ref, l_ref, acc_ref):
    m_old = m_ref[...]
    m_new = jnp.maximum(m_old, jnp.max(s, axis=-1, keepdims=True))
    m_safe = jnp.where(m_new == NEG_INF, 0.0, m_new)
    alpha = jnp.exp(m_old - m_safe)
    p = jnp.exp(s - m_safe)
    l_ref[...] = alpha * l_ref[...] + jnp.sum(p, axis=-1, keepdims=True)
    acc_ref[...] = alpha * acc_ref[...] + _dot_nt(p.astype(BF16), v_t)
    m_ref[...] = m_new


def _project_kernel(x_ref, g_ref, w_ref, tab_ref, gq_ref, wuq_ref, wuk_ref, gkv_ref,
                    qa_ref, qi_ref, qc_ref, qcat_ref, gate_ref, aw_ref,
                    dkv_ref, didx_ref, mla_ref, moba_ref, *, rows_t):
    h = _rms(x_ref[...], g_ref[...]).astype(BF16)
    z = _dot(h, w_ref[...])
    tab = tab_ref[...]
    cos_a, sin_a, cos_m, sin_m, cos_q, sin_q = [tab[:, i * LANES:(i + 1) * LANES] for i in range(6)]
    lane = _lane_iota(cos_a.shape)

    def rope_a(c):
        zc = z[:, c * LANES:(c + 1) * LANES]
        return zc * cos_a + _swap_half(zc, 32) * sin_a

    r = [rope_a(c) for c in range(8)]
    for c in range(2):
        qa_ref[:, c * LANES:(c + 1) * LANES] = (r[c] * 0.125).astype(BF16)
        qi_ref[:, c * LANES:(c + 1) * LANES] = (r[2 + c] * 0.125).astype(BF16)
        qc_ref[:, c * LANES:(c + 1) * LANES] = r[4 + c] * 0.125
    gate_ref[...] = z[:, 1024:2048]

    cq = _rms(z[:, 2048:2304], gq_ref[...]).astype(BF16)
    qb = _dot(cq, wuq_ref[...])
    for p in range(B_HEADS // 2):
        ql = _dot(qb[:, p * LANES:(p + 1) * LANES].astype(BF16), wuk_ref[p])
        qcat_ref[2 * p, :, 0:LANES] = ql[:, 0:LANES].astype(BF16)
        qcat_ref[2 * p + 1, :, 0:LANES] = ql[:, LANES:2 * LANES].astype(BF16)
    for c in range(2):
        zc = qb[:, 512 + c * LANES:512 + (c + 1) * LANES]
        rq = zc * cos_q + _swap_half(zc, 16) * sin_q
        for j in range(4):
            sh = rq if j == 0 else pltpu.roll(rq, LANES - ROPE_DIM_B * j, 1)
            qcat_ref[4 * c + j, :, LANES:2 * LANES] = jnp.where(lane < ROPE_DIM_B, sh, 0.0).astype(BF16)

    ckv = _rms(z[:, 2304:2432], gkv_ref[...])
    misc = z[:, 2560:2688]
    misc = misc * cos_m + _swap_half(misc, 16) * sin_m
    rolled = pltpu.roll(misc, 64, 1)
    aw_ref[...] = rolled
    dkv = jnp.where(lane < 64, r[7], rolled)
    kidx = pltpu.roll(r[7], 64, 1)
    c_v = z[:, 2432:2560]
    if rows_t:
        dkv_ref[...] = dkv.T
        didx_ref[...] = kidx.T[0:IDX_DIM, :]
        mla_ref[0:KV_LORA, :] = ckv.T
        mla_ref[KV_LORA:LAT_DIM, :] = rolled.T[0:ROPE_DIM_B, :]
        moba_ref[0:LANES, :] = r[6].T
        moba_ref[LANES:2 * LANES, :] = c_v.T
    else:
        dkv_ref[...] = dkv
        didx_ref[...] = kidx[:, 0:IDX_DIM]
        mla_ref[:, 0:KV_LORA] = ckv
        mla_ref[:, KV_LORA:LAT_DIM] = rolled[:, 0:ROPE_DIM_B]
        moba_ref[:, 0:LANES] = r[6]
        moba_ref[:, LANES:2 * LANES] = c_v


def _project(x2d, tab, g, wperm, gq, wuq, wukp, gkv, *, tm, batch_t):
    n, d = x2d.shape
    steps = n // tm
    ntab = tab.shape[0] // tm
    row = lambda w: pl.BlockSpec((tm, w), lambda i: (i, 0))
    const2 = lambda a: pl.BlockSpec(a.shape, lambda i: (0, 0))
    in_specs = [
        row(d), const2(g), const2(wperm),
        pl.BlockSpec((tm, TAB_WIDTH), lambda i: (i % ntab, 0)),
        const2(gq), const2(wuq), pl.BlockSpec(wukp.shape, lambda i: (0, 0, 0)), const2(gkv),
    ]
    out_shape = [
        jax.ShapeDtypeStruct((n, 256), BF16), jax.ShapeDtypeStruct((n, 256), BF16),
        jax.ShapeDtypeStruct((n, 256), F32), jax.ShapeDtypeStruct((B_HEADS, n, 256), BF16),
        jax.ShapeDtypeStruct((n, 1024), F32), jax.ShapeDtypeStruct((n, LANES), F32),
    ]
    out_specs = [row(256), row(256), row(256), pl.BlockSpec((B_HEADS, tm, 256), lambda i: (0, i, 0)),
                 row(1024), row(LANES)]
    feats = (2 * HEAD_DIM, IDX_DIM, LAT_DIM, 4 * HEAD_DIM)
    if batch_t is not None:
        b, t = batch_t
        per = t // tm
        for f in feats:
            out_shape.append(jax.ShapeDtypeStruct((b, f, t), F32))
            out_specs.append(pl.BlockSpec((None, f, tm), lambda i: (i // per, 0, i % per)))
    else:
        for f in feats:
            out_shape.append(jax.ShapeDtypeStruct((n, f), F32))
            out_specs.append(row(f))
    return pl.pallas_call(
        functools.partial(_project_kernel, rows_t=batch_t is not None),
        grid=(steps,), in_specs=in_specs, out_specs=out_specs, out_shape=out_shape,
        compiler_params=pltpu.CompilerParams(dimension_semantics=("arbitrary",), vmem_limit_bytes=VMEM_LIMIT_BYTES),
        name="project",
    )(x2d, g, wperm, tab, gq, wuq, wukp, gkv)


def _outproj_kernel(ao_ref, bl_ref, co_ref, gate_ref, x_ref, wuv_ref, wout_ref, fg_ref, o_ref, mix_ref, *, final):
    gate = gate_ref[...]
    sg = gate * jax.nn.sigmoid(gate)
    mix_ref[:, 0:256] = (ao_ref[...] * sg[:, 0:256]).astype(BF16)
    bl = bl_ref[...].astype(BF16)
    for p in range(B_HEADS // 2):
        bo = _dot(bl[:, p * 256:(p + 1) * 256], wuv_ref[p])
        mix_ref[:, 256 + p * LANES:256 + (p + 1) * LANES] = (bo * sg[:, 256 + p * LANES:256 + (p + 1) * LANES]).astype(BF16)
    mix_ref[:, 768:1024] = (co_ref[...] * sg[:, 768:1024]).astype(BF16)
    xo = x_ref[...] + _dot(mix_ref[...], wout_ref[...])
    if final:
        xo = _rms(xo, fg_ref[...])
    o_ref[...] = xo


def _outproj(ao, bl, co, gate, x2d, wuvp, wout, fg, *, tm, final):
    n, d = x2d.shape
    row = lambda w: pl.BlockSpec((tm, w), lambda i: (i, 0))
    return pl.pallas_call(
        functools.partial(_outproj_kernel, final=final),
        grid=(n // tm,),
        in_specs=[row(256), row(1024), row(256), row(1024), row(d),
                  pl.BlockSpec(wuvp.shape, lambda i: (0, 0, 0)), pl.BlockSpec(wout.shape, lambda i: (0, 0)),
                  pl.BlockSpec(fg.shape, lambda i: (0, 0))],
        out_specs=row(d), out_shape=jax.ShapeDtypeStruct((n, d), F32),
        scratch_shapes=[pltpu.VMEM((tm, 1024), BF16)],
        compiler_params=pltpu.CompilerParams(dimension_semantics=("arbitrary",), vmem_limit_bytes=VMEM_LIMIT_BYTES),
        name="outproj",
    )(ao, bl, co, gate, x2d, wuvp, wout, fg)


def _key_buckets(t, tq):
    step = max(tq, t // 4)
    return step, tuple(step * (j + 1) for j in range(t // step))


def _for_bucket(i, t, tq, body):
    step, buckets = _key_buckets(t, tq)
    bidx = ((i + 1) * tq + step - 1) // step - 1
    for j, kl in enumerate(buckets):
        pl.when(bidx == j)(functools.partial(body, kl))


def _dsa_prompt_kernel(qa_ref, qi_ref, aw_ref, kv_ref, ki_ref, o_ref, kih_ref, kh_ref, vh_ref, *, tq, t, ktop):
    i = pl.program_id(1)

    @pl.when(i == 0)
    def _():
        kv = kv_ref[...]
        k_t = kv[0:HEAD_DIM].astype(BF16)
        v_t = kv[HEAD_DIM:2 * HEAD_DIM].astype(BF16)
        ki_t = ki_ref[...].astype(BF16)
        zero = jnp.zeros((A_HEADS * HEAD_DIM, t), BF16)
        for h in range(A_HEADS):
            kih_ref[h] = zero
            kh_ref[h] = zero
            vh_ref[h] = zero
            kih_ref[h, h * HEAD_DIM:(h + 1) * HEAD_DIM, :] = ki_t
            kh_ref[h, h * HEAD_DIM:(h + 1) * HEAD_DIM, :] = k_t
            vh_ref[h, h * HEAD_DIM:(h + 1) * HEAD_DIM, :] = v_t

    def body(kl):
        qi = qi_ref[...]
        aw = aw_ref[...]
        row = i * tq + _row_iota((tq, kl))
        col = _lane_iota((tq, kl))
        causal = col <= row
        score = jnp.zeros((tq, kl), F32)
        for h in range(IDX_HEADS):
            rel = jnp.maximum(_dot(qi, kih_ref[h, :, 0:kl]), 0.0)
            score = score + rel * (aw[:, ROPE_DIM_B + h:ROPE_DIM_B + h + 1] * (IDX_HEADS ** -0.5))
        score = jnp.where(score == 0.0, 0.0, score)
        score = jnp.where(causal, score, NEG_INF)
        sel = _topk_mask(score, ktop) & causal
        qa = qa_ref[...]
        out = jnp.zeros((tq, A_HEADS * HEAD_DIM), F32)
        for h in range(A_HEADS):
            out = out + _softmax_pv(_dot(qa, kh_ref[h, :, 0:kl]), sel, vh_ref[h, :, 0:kl])
        o_ref[...] = out

    _for_bucket(i, t, tq, body)


def _dsa_prompt(qa, qi, aw, dkv_t, didx_t, *, tq):
    b, _, t = dkv_t.shape
    nq = t // tq
    qrow = lambda w: pl.BlockSpec((tq, w), lambda bb, i: (bb * nq + i, 0))
    return pl.pallas_call(
        functools.partial(_dsa_prompt_kernel, tq=tq, t=t, ktop=min(DSA_TOPK, t // 4)),
        grid=(b, nq),
        in_specs=[qrow(256), qrow(256), qrow(LANES),
                  pl.BlockSpec((None, 2 * HEAD_DIM, t), lambda bb, i: (bb, 0, 0)),
                  pl.BlockSpec((None, IDX_DIM, t), lambda bb, i: (bb, 0, 0))],
        out_specs=qrow(256), out_shape=jax.ShapeDtypeStruct((b * t, 256), F32),
        scratch_shapes=[pltpu.VMEM((A_HEADS, 256, t), BF16)] * 3,
        compiler_params=pltpu.CompilerParams(dimension_semantics=("arbitrary", "arbitrary"),
                                             vmem_limit_bytes=VMEM_LIMIT_BYTES),
        name="dsa_prompt",
    )(qa, qi, aw, dkv_t, didx_t)


def _mla_prompt_kernel(qcat_ref, lat_ref, o_ref, kp_ref, *, tq, t):
    i = pl.program_id(1)

    @pl.when(i == 0)
    def _():
        kp_ref[0:LAT_DIM, :] = lat_ref[...].astype(BF16)
        kp_ref[LAT_DIM:256, :] = jnp.zeros((256 - LAT_DIM, t), BF16)

    def body(kl):
        causal = _lane_iota((tq, kl)) <= i * tq + _row_iota((tq, kl))
        for h in range(B_HEADS):
            s = _dot(qcat_ref[h], kp_ref[:, 0:kl]) * MLA_SCALE
            o_ref[:, h * KV_LORA:(h + 1) * KV_LORA] = _softmax_pv(s, causal, kp_ref[0:KV_LORA, 0:kl])

    _for_bucket(i, t, tq, body)


def _mla_prompt(qcat, mla_t, *, tq):
    b, _, t = mla_t.shape
    nq = t // tq
    return pl.pallas_call(
        functools.partial(_mla_prompt_kernel, tq=tq, t=t),
        grid=(b, nq),
        in_specs=[pl.BlockSpec((B_HEADS, tq, 256), lambda bb, i: (0, bb * nq + i, 0)),
                  pl.BlockSpec((None, LAT_DIM, t), lambda bb, i: (bb, 0, 0))],
        out_specs=pl.BlockSpec((tq, B_HEADS * KV_LORA), lambda bb, i: (bb * nq + i, 0)),
        out_shape=jax.ShapeDtypeStruct((b * t, B_HEADS * KV_LORA), F32),
        scratch_shapes=[pltpu.VMEM((256, t), BF16)],
        compiler_params=pltpu.CompilerParams(dimension_semantics=("arbitrary", "arbitrary"),
                                             vmem_limit_bytes=VMEM_LIMIT_BYTES),
        name="mla_prompt",
    )(qcat, mla_t)


def _moba_prompt_kernel(qc_ref, rows_ref, o_ref, kh_ref, vh_ref, kmt_ref, exp_ref, *, tq, t, n_sel):
    i = pl.program_id(1)
    nb = t // MOBA_BLOCK

    @pl.when(i == 0)
    def _():
        rows = rows_ref[...]
        lane = _lane_iota((2 * HEAD_DIM, LANES))
        km = jnp.zeros((2 * HEAD_DIM, LANES), F32)
        for n in range(nb):
            col = jnp.sum(rows[0:2 * HEAD_DIM, n * MOBA_BLOCK:(n + 1) * MOBA_BLOCK], axis=1, keepdims=True)
            km = jnp.where(lane == n, col * (1.0 / MOBA_BLOCK), km)
        zero = jnp.zeros((C_HEADS * HEAD_DIM, t), BF16)
        for h in range(C_HEADS):
            g = h // (C_HEADS // C_KV_HEADS)
            kh_ref[h] = zero
            vh_ref[h] = zero
            kh_ref[h, h * HEAD_DIM:(h + 1) * HEAD_DIM, :] = rows[g * HEAD_DIM:(g + 1) * HEAD_DIM].astype(BF16)
            vh_ref[h, h * HEAD_DIM:(h + 1) * HEAD_DIM, :] = rows[(2 + g) * HEAD_DIM:(3 + g) * HEAD_DIM].astype(BF16)
            kmt_ref[h] = jnp.zeros((C_HEADS * HEAD_DIM, LANES), F32)
            kmt_ref[h, h * HEAD_DIM:(h + 1) * HEAD_DIM, :] = km[g * HEAD_DIM:(g + 1) * HEAD_DIM]
        blk = _row_iota((LANES, t))
        exp_ref[...] = jnp.where(_lane_iota((LANES, t)) // MOBA_BLOCK == blk, 1.0, 0.0).astype(BF16)

    def body(kl):
        q = qc_ref[...]
        qb = q.astype(BF16)
        row = i * tq + _row_iota((tq, kl))
        col = _lane_iota((tq, kl))
        own_keys = (col // MOBA_BLOCK == row // MOBA_BLOCK) & (col <= row)
        own = (i * tq + _row_iota((tq, LANES))) // MOBA_BLOCK
        past = _lane_iota((tq, LANES)) < own
        out = jnp.zeros((tq, C_HEADS * HEAD_DIM), F32)
        for h in range(C_HEADS):
            gate = _dot(q, kmt_ref[h], precision=HIGHEST)
            selb = jnp.where(_topn_lanes(gate, past, n_sel), 1.0, 0.0).astype(BF16)
            allowed = (_dot(selb, exp_ref[:, 0:kl]) > 0.5) | own_keys
            out = out + _softmax_pv(_dot(qb, kh_ref[h, :, 0:kl]), allowed, vh_ref[h, :, 0:kl])
        o_ref[...] = out

    _for_bucket(i, t, tq, body)


def _moba_prompt(qc, moba_t, *, tq):
    b, _, t = moba_t.shape
    nq = t // tq
    nb = t // MOBA_BLOCK
    qrow = lambda w: pl.BlockSpec((tq, w), lambda bb, i: (bb * nq + i, 0))
    return pl.pallas_call(
        functools.partial(_moba_prompt_kernel, tq=tq, t=t, n_sel=min(MOBA_TOPK, nb - 1)),
        grid=(b, nq),
        in_specs=[qrow(256), pl.BlockSpec((None, 4 * HEAD_DIM, t), lambda bb, i: (bb, 0, 0))],
        out_specs=qrow(256), out_shape=jax.ShapeDtypeStruct((b * t, 256), F32),
        scratch_shapes=[pltpu.VMEM((C_HEADS, 256, t), BF16), pltpu.VMEM((C_HEADS, 256, t), BF16),
                        pltpu.VMEM((C_HEADS, 256, LANES), F32), pltpu.VMEM((LANES, t), BF16)],
        compiler_params=pltpu.CompilerParams(dimension_semantics=("arbitrary", "arbitrary"),
                                             vmem_limit_bytes=VMEM_LIMIT_BYTES),
        name="moba_prompt",
    )(qc, moba_t)


def _page_specs(layer, feat, group):
    return [
        pl.BlockSpec((None, None, feat, LANES),
                     functools.partial(lambda b, g, pt, j: (layer, pt[b, g * group + j], 0, 0), j=j))
        for j in range(group)
    ]


def _per_batch(shape):
    return pl.BlockSpec((None,) + tuple(shape), lambda b, g, pt: (b,) + (0,) * len(shape))


def _dsa_score_kernel(pt_ref, qi_ref, w_ref, new_ref, *rest, group):
    pages = rest[:group]
    past_ref, newsc_ref, ki_ref = rest[group:]
    g = pl.program_id(1)
    q = qi_ref[...].astype(BF16)
    w = w_ref[...] * (IDX_HEADS ** -0.5)
    nt = q.shape[0] // IDX_HEADS

    def head_sum(keys_t):
        rel = jnp.maximum(_dot(q, keys_t), 0.0) * w
        acc = rel[0:nt]
        for h in range(1, IDX_HEADS):
            acc = acc + rel[h * nt:(h + 1) * nt]
        return acc

    for j in range(group):
        ki_ref[:, j * LANES:(j + 1) * LANES] = pages[j][...].astype(BF16)
    past_ref[...] = head_sum(ki_ref[...])

    @pl.when(g == 0)
    def _():
        newsc_ref[...] = head_sum(new_ref[...].astype(BF16))


def _dsa_score(layer, page_table, idx_cache_t, qi_s, w_s, new_idx_t, *, group):
    b, n_pages = page_table.shape
    nt = qi_s.shape[1] // IDX_HEADS
    past = n_pages * LANES
    return pl.pallas_call(
        functools.partial(_dsa_score_kernel, group=group),
        grid_spec=pltpu.PrefetchScalarGridSpec(
            num_scalar_prefetch=1, grid=(b, n_pages // group),
            in_specs=[_per_batch(qi_s.shape[1:]), _per_batch(w_s.shape[1:]), _per_batch(new_idx_t.shape[1:])]
            + _page_specs(layer, IDX_DIM, group),
            out_specs=[pl.BlockSpec((None, nt, group * LANES), lambda bb, g, pt: (bb, 0, g)),
                       _per_batch((nt, LANES))],
            scratch_shapes=[pltpu.VMEM((IDX_DIM, group * LANES), BF16)]),
        out_shape=[jax.ShapeDtypeStruct((b, nt, past), F32), jax.ShapeDtypeStruct((b, nt, LANES), F32)],
        compiler_params=pltpu.CompilerParams(dimension_semantics=("arbitrary", "arbitrary"),
                                             vmem_limit_bytes=VMEM_LIMIT_BYTES),
        name="dsa_sample_score",
    )(page_table, qi_s, w_s, new_idx_t, *([idx_cache_t] * group))


def _dsa_select_kernel(past_ref, new_ref, mask_ref, *, ktop):
    rb, nt, past = past_ref.shape
    rows = rb * nt
    score = jnp.concatenate([past_ref[...].reshape(rows, past), new_ref[...].reshape(rows, LANES)], axis=1)
    idx = _lane_iota(score.shape)
    tok = _row_iota(score.shape) % nt
    valid = (idx < past) | (idx - past <= tok)
    score = jnp.where(score == 0.0, 0.0, score)
    score = jnp.where(valid, score, NEG_INF)
    sel = _topk_mask(score, ktop) & valid
    mask_ref[...] = jnp.where(sel, 1.0, 0.0).reshape(rb, nt, past + LANES)


def _dsa_select(sc_past, sc_new, *, rb):
    b, nt, past = sc_past.shape
    return pl.pallas_call(
        functools.partial(_dsa_select_kernel, ktop=min(DSA_TOPK, (past + nt) // 4)),
        grid=(b // rb,),
        in_specs=[pl.BlockSpec((rb, nt, past), lambda i: (i, 0, 0)), pl.BlockSpec((rb, nt, LANES), lambda i: (i, 0, 0))],
        out_specs=pl.BlockSpec((rb, nt, past + LANES), lambda i: (i, 0, 0)),
        out_shape=jax.ShapeDtypeStruct((b, nt, past + LANES), F32),
        compiler_params=pltpu.CompilerParams(dimension_semantics=("arbitrary",), vmem_limit_bytes=VMEM_LIMIT_BYTES),
        name="dsa_sample_select",
    )(sc_past, sc_new)


def _dsa_attn_kernel(pt_ref, q_ref, mpast_ref, mnew_ref, new_ref, *rest, group):
    pages = rest[:group]
    o_ref, k_ref, v_ref, m_ref, l_ref, acc_ref = rest[group:]
    g = pl.program_id(1)
    q = q_ref[...].astype(BF16)

    def masked(s, mask):
        allowed = jnp.concatenate([mask] * A_HEADS, axis=0) > 0.5
        return jnp.where(allowed, s, NEG_INF)

    @pl.when(g == 0)
    def _():
        m_ref[...] = jnp.full(m_ref.shape, NEG_INF, F32)
        l_ref[...] = jnp.zeros(l_ref.shape, F32)
        acc_ref[...] = jnp.zeros(acc_ref.shape, F32)
        new = new_ref[...].astype(BF16)
        _flash_update(masked(_dot(q, new[0:HEAD_DIM]), mnew_ref[...]), new[HEAD_DIM:2 * HEAD_DIM],
                      m_ref, l_ref, acc_ref)

    for j in range(group):
        page = pages[j][...].astype(BF16)
        k_ref[:, j * LANES:(j + 1) * LANES] = page[0:HEAD_DIM]
        v_ref[:, j * LANES:(j + 1) * LANES] = page[HEAD_DIM:2 * HEAD_DIM]
    _flash_update(masked(_dot(q, k_ref[...]), mpast_ref[...]), v_ref[...], m_ref, l_ref, acc_ref)

    @pl.when(g == pl.num_programs(1) - 1)
    def _():
        o_ref[...] = acc_ref[...] / l_ref[...]


def _dsa_attn(layer, page_table, kv_cache_t, qa_s, mask, new_kv_t, *, group):
    b, n_pages = page_table.shape
    rows = qa_s.shape[1]
    nt = rows // A_HEADS
    return pl.pallas_call(
        functools.partial(_dsa_attn_kernel, group=group),
        grid_spec=pltpu.PrefetchScalarGridSpec(
            num_scalar_prefetch=1, grid=(b, n_pages // group),
            in_specs=[_per_batch(qa_s.shape[1:]),
                      pl.BlockSpec((None, nt, group * LANES), lambda bb, g, pt: (bb, 0, g)),
                      pl.BlockSpec((None, nt, LANES), lambda bb, g, pt: (bb, 0, n_pages)),
                      _per_batch(new_kv_t.shape[1:])] + _page_specs(layer, 2 * HEAD_DIM, group),
            out_specs=_per_batch((rows, HEAD_DIM)),
            scratch_shapes=[pltpu.VMEM((HEAD_DIM, group * LANES), BF16), pltpu.VMEM((HEAD_DIM, group * LANES), BF16),
                            pltpu.VMEM((rows, 1), F32), pltpu.VMEM((rows, 1), F32), pltpu.VMEM((rows, HEAD_DIM), F32)]),
        out_shape=jax.ShapeDtypeStruct((b, rows, HEAD_DIM), F32),
        compiler_params=pltpu.CompilerParams(dimension_semantics=("arbitrary", "arbitrary"),
                                             vmem_limit_bytes=VMEM_LIMIT_BYTES),
        name="dsa_sample_attn",
    )(page_table, qa_s, mask, mask, new_kv_t, *([kv_cache_t] * group))


def _mla_sample_kernel(pt_ref, q_ref, new_ref, *rest, group):
    pages = rest[:group]
    o_ref, k_ref, m_ref, l_ref, acc_ref = rest[group:]
    g = pl.program_id(1)
    q = q_ref[...].astype(BF16)
    rows = q.shape[0]
    nt = rows // B_HEADS

    @pl.when(g == 0)
    def _():
        m_ref[...] = jnp.full(m_ref.shape, NEG_INF, F32)
        l_ref[...] = jnp.zeros(l_ref.shape, F32)
        acc_ref[...] = jnp.zeros(acc_ref.shape, F32)
        k_ref[LAT_DIM:256, :] = jnp.zeros((256 - LAT_DIM, group * LANES), BF16)
        k_ref[0:LAT_DIM, 0:LANES] = new_ref[...].astype(BF16)
        s = _dot(q, k_ref[:, 0:LANES]) * MLA_SCALE
        causal = _lane_iota((rows, LANES)) <= _row_iota((rows, LANES)) % nt
        _flash_update(jnp.where(causal, s, NEG_INF), k_ref[0:KV_LORA, 0:LANES], m_ref, l_ref, acc_ref)

    for j in range(group):
        k_ref[0:LAT_DIM, j * LANES:(j + 1) * LANES] = pages[j][...].astype(BF16)
    _flash_update(_dot(q, k_ref[...]) * MLA_SCALE, k_ref[0:KV_LORA, :], m_ref, l_ref, acc_ref)

    @pl.when(g == pl.num_programs(1) - 1)
    def _():
        o_ref[...] = acc_ref[...] / l_ref[...]


def _mla_sample(layer, page_table, mla_cache_t, qcat_s, new_mla_t, *, group):
    b, n_pages = page_table.shape
    rows = qcat_s.shape[1]
    return pl.pallas_call(
        functools.partial(_mla_sample_kernel, group=group),
        grid_spec=pltpu.PrefetchScalarGridSpec(
            num_scalar_prefetch=1, grid=(b, n_pages // group),
            in_specs=[_per_batch(qcat_s.shape[1:]), _per_batch(new_mla_t.shape[1:])]
            + _page_specs(layer, LAT_DIM, group),
            out_specs=_per_batch((rows, KV_LORA)),
            scratch_shapes=[pltpu.VMEM((256, group * LANES), BF16),
                            pltpu.VMEM((rows, 1), F32), pltpu.VMEM((rows, 1), F32), pltpu.VMEM((rows, KV_LORA), F32)]),
        out_shape=jax.ShapeDtypeStruct((b, rows, KV_LORA), F32),
        compiler_params=pltpu.CompilerParams(dimension_semantics=("arbitrary", "arbitrary"),
                                             vmem_limit_bytes=VMEM_LIMIT_BYTES),
        name="mla_sample",
    )(page_table, qcat_s, new_mla_t, *([mla_cache_t] * group))


def _moba_sample_kernel(pt_ref, q_ref, new_ref, *rest, group, n_past, n_sel):
    pages = rest[:group]
    o_ref, km_ref, ms_ref, ls_ref, ob_ref = rest[group:]
    g = pl.program_id(1)
    q = q_ref[...]
    qb = q.astype(BF16)
    rows = q.shape[0]
    nt = rows // C_HEADS
    half = 2 * HEAD_DIM
    lane = _lane_iota((rows, LANES))
    per_step = group // 2

    @pl.when(g == 0)
    def _():
        km_ref[...] = jnp.zeros(km_ref.shape, F32)
        ms_ref[...] = jnp.zeros(ms_ref.shape, F32)
        ls_ref[...] = jnp.zeros(ls_ref.shape, F32)

    for jj in range(per_step):
        n = g * per_step + jj
        pa = pages[2 * jj][...]
        pb = pages[2 * jj + 1][...]
        k_t = jnp.concatenate([pa[0:half], pb[0:half]], axis=1)
        v_t = jnp.concatenate([pa[half:2 * half], pb[half:2 * half]], axis=1)
        kmean = jnp.sum(k_t, axis=1, keepdims=True) * (1.0 / MOBA_BLOCK)
        km_ref[...] = jnp.where(_lane_iota((half, LANES)) == n, kmean, km_ref[...])
        s = _dot(qb, k_t.astype(BF16))
        m = jnp.max(s, axis=-1, keepdims=True)
        p = jnp.exp(s - m)
        ms_ref[...] = jnp.where(lane == n, m, ms_ref[...])
        ls_ref[...] = jnp.where(lane == n, jnp.sum(p, axis=-1, keepdims=True), ls_ref[...])
        ob_ref[n] = _dot_nt(p.astype(BF16), v_t.astype(BF16))

    @pl.when(g == pl.num_programs(1) - 1)
    def _():
        gate = _dot(q, km_ref[...], precision=HIGHEST)
        sel = _topn_lanes(gate, lane < n_past, n_sel)
        new = new_ref[...].astype(BF16)
        s_own = jnp.where(lane <= _row_iota((rows, LANES)) % nt, _dot(qb, new[0:half]), NEG_INF)
        m_own = jnp.max(s_own, axis=-1, keepdims=True)
        m_sel = jnp.where(sel, ms_ref[...], NEG_INF)
        m_tot = jnp.maximum(jnp.max(m_sel, axis=-1, keepdims=True), m_own)
        p_own = jnp.exp(s_own - m_tot)
        wts = jnp.exp(m_sel - m_tot)
        l_tot = jnp.sum(wts * ls_ref[...], axis=-1, keepdims=True) + jnp.sum(p_own, axis=-1, keepdims=True)
        acc = _dot_nt(p_own.astype(BF16), new[half:2 * half])
        for n in range(n_past):
            acc = acc + wts[:, n:n + 1] * ob_ref[n]
        out = acc / l_tot
        upper = _row_iota((rows, LANES)) >= (C_HEADS // C_KV_HEADS) * nt
        o_ref[...] = jnp.where(upper, pltpu.roll(out, HEAD_DIM, 1), out)


def _moba_sample(layer, page_table, moba_cache_t, q2_s, new_moba_t, *, group):
    b, n_pages = page_table.shape
    rows = q2_s.shape[1]
    n_past = n_pages * LANES // MOBA_BLOCK
    return pl.pallas_call(
        functools.partial(_moba_sample_kernel, group=group, n_past=n_past, n_sel=min(MOBA_TOPK, n_past)),
        grid_spec=pltpu.PrefetchScalarGridSpec(
            num_scalar_prefetch=1, grid=(b, n_pages // group),
            in_specs=[_per_batch(q2_s.shape[1:]), _per_batch(new_moba_t.shape[1:])]
            + _page_specs(layer, 4 * HEAD_DIM, group),
            out_specs=_per_batch((rows, LANES)),
            scratch_shapes=[pltpu.VMEM((2 * HEAD_DIM, LANES), F32), pltpu.VMEM((rows, LANES), F32),
                            pltpu.VMEM((rows, LANES), F32), pltpu.VMEM((n_past, rows, LANES), F32)]),
        out_shape=jax.ShapeDtypeStruct((b, rows, LANES), F32),
        compiler_params=pltpu.CompilerParams(dimension_semantics=("arbitrary", "arbitrary"),
                                             vmem_limit_bytes=VMEM_LIMIT_BYTES),
        name="moba_sample",
    )(page_table, q2_s, new_moba_t, *([moba_cache_t] * group))


def _rope_tables(pos):
    posf = pos.astype(F32)[:, None]

    def cos_sin(half):
        inv_freq = ROPE_THETA ** (-jnp.arange(half, dtype=F32) / half)
        ang = posf * inv_freq[None, :]
        return jnp.cos(ang), jnp.sin(ang)

    c32, s32 = cos_sin(HEAD_DIM // 2)
    c16, s16 = cos_sin(ROPE_DIM_B // 2)
    cos_a = jnp.tile(c32, (1, 4))
    sin_a = jnp.tile(jnp.concatenate([-s32, s32], axis=1), (1, 2))
    cos_q = jnp.tile(c16, (1, 8))
    sin_q = jnp.tile(jnp.concatenate([-s16, s16], axis=1), (1, 4))
    lane = jnp.arange(LANES)[None, :]
    in_kr = (lane >= 64) & (lane < 64 + ROPE_DIM_B)
    cos_m = jnp.where(in_kr, cos_q, 1.0)
    sin_m = jnp.where(in_kr, sin_q, 0.0)
    return jnp.concatenate([cos_a, sin_a, cos_m, sin_m, cos_q, sin_q], axis=1)


def _layer_weights(layer, norm_g, w_in, w_out, mla_g_q, mla_w_uq, mla_g_kv, mla_w_uk, mla_w_uv):
    w = w_in[layer]
    cols = [w[:, _OFF[n][0]:_OFF[n][1]] for n in _PERM]
    used = sum(c.shape[1] for c in cols)
    cols.append(jnp.zeros((w.shape[0], PROJ_WIDTH - used), w.dtype))
    wperm = jnp.concatenate(cols, axis=1).astype(BF16)
    uq = mla_w_uq[layer]
    wuq = jnp.concatenate([uq[:, :, :NOPE_DIM].reshape(Q_LORA, -1), uq[:, :, NOPE_DIM:].reshape(Q_LORA, -1)],
                          axis=1).astype(BF16)
    uk = jnp.transpose(mla_w_uk[layer], (1, 2, 0))
    uv = jnp.transpose(mla_w_uv[layer], (1, 0, 2))
    zk = jnp.zeros((NOPE_DIM, KV_LORA), F32)
    zv = jnp.zeros((KV_LORA, V_DIM_B), F32)
    wukp = jnp.stack([jnp.block([[uk[2 * p], zk], [zk, uk[2 * p + 1]]]) for p in range(B_HEADS // 2)]).astype(BF16)
    wuvp = jnp.stack([jnp.block([[uv[2 * p], zv], [zv, uv[2 * p + 1]]]) for p in range(B_HEADS // 2)]).astype(BF16)
    return dict(g=norm_g[layer][None, :], wperm=wperm, gq=mla_g_q[layer][None, :], wuq=wuq, wukp=wukp,
                gkv=mla_g_kv[layer][None, :], wuvp=wuvp, wout=w_out[layer].astype(BF16))


def _heads_first(a, b, nt, heads):
    d = a.shape[-1] // heads
    return a.reshape(b, nt, heads, d).transpose(0, 2, 1, 3).reshape(b, heads * nt, d)


def _tokens_first(a, b, nt, heads):
    d = a.shape[-1]
    return a.reshape(b, heads, nt, d).transpose(0, 2, 1, 3).reshape(b * nt, heads * d)


def _new_page(rows, b, nt):
    f = rows.shape[-1]
    return jnp.pad(rows.reshape(b, nt, f).transpose(0, 2, 1), ((0, 0), (0, 0), (0, LANES - nt)))


def kernel(x_prompt, x_sample, cache_dsa_kv, cache_dsa_idx, cache_mla, cache_moba_kv, page_table, norm_g, w_in, w_out, mla_g_q, mla_w_uq, mla_g_kv, mla_w_uk, mla_w_uv, final_g):
    depth = w_in.shape[0]
    bp, t, d = x_prompt.shape
    bs, nt, _ = x_sample.shape
    n_phys, page = cache_mla.shape[1], cache_mla.shape[2]
    n_pages = page_table.shape[1]
    past = n_pages * page
    assert page == LANES and t % MOBA_BLOCK == 0 and past % MOBA_BLOCK == 0 and nt == 8
    tm_p = 512
    tm_s = min(512, bs * nt)
    tq = 256
    group = min(16, n_pages)
    rb = min(4, bs)

    tab_p = _rope_tables(jnp.arange(t, dtype=jnp.int32))
    tab_s = jnp.tile(_rope_tables(past + jnp.arange(nt, dtype=jnp.int32)), (tm_s // nt, 1))
    fg = final_g[None, :]

    idx_t = jnp.transpose(cache_dsa_idx, (0, 1, 3, 2))
    kv_t = jnp.transpose(cache_dsa_kv, (0, 1, 3, 4, 2)).reshape(depth, n_phys, 2 * HEAD_DIM, page)
    mla_t = jnp.transpose(cache_mla, (0, 1, 3, 2))
    moba_t = jnp.transpose(cache_moba_kv, (0, 1, 3, 4, 5, 2)).reshape(depth, n_phys, 4 * HEAD_DIM, page)

    xp = x_prompt.reshape(bp * t, d)
    xs = x_sample.reshape(bs * nt, d)
    rows_p, rows_s = [], []
    for layer in range(depth):
        wl = _layer_weights(layer, norm_g, w_in, w_out, mla_g_q, mla_w_uq, mla_g_kv, mla_w_uk, mla_w_uv)
        final = layer == depth - 1

        qa, qi, qc, qcat, gate, aw, dkv_t, didx_t, lat_t, mob_t = _project(
            xp, tab_p, wl["g"], wl["wperm"], wl["gq"], wl["wuq"], wl["wukp"], wl["gkv"], tm=tm_p, batch_t=(bp, t))
        a_o = _dsa_prompt(qa, qi, aw, dkv_t, didx_t, tq=tq)
        b_lat = _mla_prompt(qcat, lat_t, tq=tq)
        c_o = _moba_prompt(qc, mob_t, tq=tq)
        xp = _outproj(a_o, b_lat, c_o, gate, xp, wl["wuvp"], wl["wout"], fg, tm=tm_p, final=final)
        rows_p.append((dkv_t, didx_t, lat_t, mob_t))

        qa, qi, qc, qcat, gate, aw, dkv, didx, lat, mob = _project(
            xs, tab_s, wl["g"], wl["wperm"], wl["gq"], wl["wuq"], wl["wukp"], wl["gkv"], tm=tm_s, batch_t=None)
        qa_s = _heads_first(qa.astype(F32), bs, nt, A_HEADS)
        qi_s = _heads_first(qi.astype(F32), bs, nt, IDX_HEADS)
        w_s = _heads_first(aw[:, ROPE_DIM_B:ROPE_DIM_B + IDX_HEADS], bs, nt, IDX_HEADS)
        qcat_s = qcat.astype(F32).reshape(B_HEADS, bs, nt, 256).transpose(1, 0, 2, 3).reshape(bs, B_HEADS * nt, 256)
        qc_h = _heads_first(qc, bs, nt, C_HEADS).reshape(bs, C_KV_HEADS, (C_HEADS // C_KV_HEADS) * nt, HEAD_DIM)
        zq = jnp.zeros_like(qc_h[:, 0])
        q2_s = jnp.concatenate([jnp.concatenate([qc_h[:, 0], zq], axis=-1),
                                jnp.concatenate([zq, qc_h[:, 1]], axis=-1)], axis=1)
        sc_past, sc_new = _dsa_score(layer, page_table, idx_t, qi_s, w_s, _new_page(didx, bs, nt), group=group)
        mask = _dsa_select(sc_past, sc_new, rb=rb)
        a_o = _dsa_attn(layer, page_table, kv_t, qa_s, mask, _new_page(dkv, bs, nt), group=group)
        b_lat = _mla_sample(layer, page_table, mla_t, qcat_s, _new_page(lat, bs, nt), group=group)
        c_o = _moba_sample(layer, page_table, moba_t, q2_s, _new_page(mob, bs, nt), group=group)
        xs = _outproj(_tokens_first(a_o, bs, nt, A_HEADS), _tokens_first(b_lat, bs, nt, B_HEADS),
                      _tokens_first(c_o[:, :, :HEAD_DIM], bs, nt, C_HEADS), gate, xs,
                      wl["wuvp"], wl["wout"], fg, tm=tm_s, final=final)
        rows_s.append((dkv, didx, lat, mob))

    def prompt_rows(k, feat_shape):
        a = jnp.stack([r[k] for r in rows_p])
        a = a.reshape((depth, bp) + feat_shape + (t,))
        nd = a.ndim
        return jnp.transpose(a, (0, 1, nd - 1) + tuple(range(2, nd - 1)))

    def sample_rows(k, feat_shape):
        return jnp.stack([r[k] for r in rows_s]).reshape((depth, bs, nt) + feat_shape)

    shapes = ((2, HEAD_DIM), (IDX_DIM,), (LAT_DIM,), (2, C_KV_HEADS, HEAD_DIM))
    return (xp.reshape(bp, t, d), xs.reshape(bs, nt, d),
            *[prompt_rows(k, s) for k, s in enumerate(shapes)],
            *[sample_rows(k, s) for k, s in enumerate(shapes)])
```

```python
import functools

import jax
import jax.numpy as jnp
from jax import lax
from jax.experimental import pallas as pl
from jax.experimental.pallas import tpu as pltpu

F32 = jnp.float32
BF16 = jnp.bfloat16
NEG_INF = float("-inf")
HIGHEST = lax.Precision.HIGHEST

HEAD_DIM = 64
ROPE_THETA = 10000.0
NORM_EPS = 1e-6
A_HEADS = 4
IDX_HEADS = 4
IDX_DIM = 64
DSA_TOPK = 256
B_HEADS = 8
Q_LORA = 256
KV_LORA = 128
NOPE_DIM = 64
ROPE_DIM_B = 32
V_DIM_B = 64
MLA_SCALE = (NOPE_DIM + ROPE_DIM_B) ** -0.5
C_HEADS = 4
C_KV_HEADS = 2
MOBA_BLOCK = 256
MOBA_TOPK = 3
LAT_DIM = KV_LORA + ROPE_DIM_B

LANES = 128
VMEM_LIMIT_BYTES = 56 * 1024 * 1024

_SPLITS = (
    ("a_q", 256), ("a_k", 64), ("a_v", 64), ("a_qi", 256), ("a_ki", 64), ("a_w", 4), ("a_gate", 256),
    ("b_cq", 256), ("b_ckv", 128), ("b_kr", 32), ("b_gate", 512),
    ("c_q", 256), ("c_k", 128), ("c_v", 128), ("c_gate", 256),
)
_OFF = {}
_o = 0
for _n, _w in _SPLITS:
    _OFF[_n] = (_o, _o + _w)
    _o += _w
IN_WIDTH = _o
_PERM = ("a_q", "a_qi", "c_q", "c_k", "a_k", "a_ki", "a_gate", "b_gate", "c_gate", "b_cq", "b_ckv", "c_v",
         "a_v", "b_kr", "a_w")
PROJ_WIDTH = 2688
TAB_WIDTH = 6 * LANES


def _dot(a, b, precision=None):
    return jnp.dot(a, b, preferred_element_type=F32, precision=precision)


def _dot_nt(a, b):
    return lax.dot_general(a, b, (((1,), (1,)), ((), ())), preferred_element_type=F32)


def _lane_iota(shape):
    return lax.broadcasted_iota(jnp.int32, shape, len(shape) - 1)


def _row_iota(shape):
    return lax.broadcasted_iota(jnp.int32, shape, len(shape) - 2)


def _swap_half(x, half):
    n = x.shape[-1]
    lane = _lane_iota(x.shape)
    fwd = pltpu.roll(x, n - half, 1)
    bwd = pltpu.roll(x, half, 1)
    return jnp.where((lane & (2 * half - 1)) < half, fwd, bwd)


def _rms(x, g):
    return x * lax.rsqrt(jnp.mean(x * x, axis=-1, keepdims=True) + NORM_EPS) * g


def _topk_mask(score, k):
    rows, width = score.shape
    bits = lax.bitcast_convert_type(score, jnp.int32)
    u = jnp.where(bits < 0, bits ^ jnp.int32(0x7FFFFFFF), bits)
    kf = float(k)

    def count_ge(t):
        return jnp.sum(jnp.where(u >= t, 1.0, 0.0), axis=-1, keepdims=True)

    zero = jnp.zeros((rows, 1), jnp.int32)
    base = jnp.where(count_ge(zero) >= kf, zero, jnp.int32(-2 ** 31))

    def value_step(it, base):
        cand = base | jnp.left_shift(jnp.int32(1), 30 - it)
        return jnp.where(count_ge(cand) >= kf, cand, base)

    thr = lax.fori_loop(0, 31, value_step, base)
    gt = u > thr
    eq = u == thr
    need = kf - jnp.sum(jnp.where(gt, 1.0, 0.0), axis=-1, keepdims=True)
    idx = _lane_iota(score.shape)
    nbits = int(width).bit_length()

    def index_step(it, lim):
        cand = lim | jnp.left_shift(jnp.int32(1), nbits - 1 - it)
        below = jnp.sum(jnp.where(eq & (idx < cand), 1.0, 0.0), axis=-1, keepdims=True)
        return jnp.where(below < need, cand, lim)

    n_eq = jnp.sum(jnp.where(eq, 1.0, 0.0), axis=-1, keepdims=True)
    tied = jnp.max(jnp.where(n_eq != need, 1.0, 0.0)) > 0.5
    lim = lax.cond(tied,
                   lambda: lax.fori_loop(0, nbits, index_step, zero),
                   lambda: jnp.full((rows, 1), 2 ** nbits - 1, jnp.int32))
    return gt | (eq & (idx <= lim))


def _topn_lanes(gate, allowed, n_sel):
    lanef = _lane_iota(gate.shape).astype(F32)
    gm = jnp.where(allowed, gate, NEG_INF)
    sel = jnp.zeros(gate.shape, jnp.bool_)
    for _ in range(n_sel):
        m = jnp.max(gm, axis=-1, keepdims=True)
        first = jnp.min(jnp.where(gm == m, lanef, float(LANES)), axis=-1, keepdims=True)
        pick = lanef == first
        sel = sel | pick
        gm = jnp.where(pick, NEG_INF, gm)
    return sel & allowed


LOG2_E = 1.4426950408889634


def _softmax_pv(pieces, scale=1.0):
    m = None
    for s, _ in pieces:
        mj = jnp.max(s, axis=-1, keepdims=True)
        m = mj if m is None else jnp.maximum(m, mj)
    l = None
    o = None
    for s, v_t in pieces:
        p = jnp.exp2((s - m) * (scale * LOG2_E))
        lj = jnp.sum(p, axis=-1, keepdims=True)
        oj = _dot_nt(p.astype(BF16), v_t)
        l = lj if l is None else l + lj
        o = oj if o is None else o + oj
    return o / l


def _causal_bias(tq):
    return jnp.where(_lane_iota((tq, tq)) <= _row_iota((tq, tq)), 0.0, NEG_INF)


def _flash_update(s, v_t, m_ref, l_ref, acc_ref):
    m_old = m_ref[...]
    m_new = jnp.maximum(m_old, jnp.max(s, axis=-1, keepdims=True))
    m_safe = jnp.where(m_new == NEG_INF, 0.0, m_new)
    alpha = jnp.exp(m_old - m_safe)
    p = jnp.exp(s - m_safe)
    l_ref[...] = alpha * l_ref[...] + jnp.sum(p, axis=-1, keepdims=True)
    acc_ref[...] = alpha * acc_ref[...] + _dot_nt(p.astype(BF16), v_t)
    m_ref[...] = m_new


def _project_kernel(x_ref, g_ref, w_ref, tab_ref, gq_ref, wuq_ref, wuk_ref, gkv_ref,
                    qa_ref, qi_ref, qc_ref, qcat_ref, gate_ref, aw_ref,
                    dkv_ref, didx_ref, mla_ref, moba_ref, *, rows_t):
    h = _rms(x_ref[...], g_ref[...]).astype(BF16)
    z = _dot(h, w_ref[...])
    tab = tab_ref[...]
    cos_a, sin_a, cos_m, sin_m, cos_q, sin_q = [tab[:, i * LANES:(i + 1) * LANES] for i in range(6)]
    lane = _lane_iota(cos_a.shape)

    def rope_a(c):
        zc = z[:, c * LANES:(c + 1) * LANES]
        return zc * cos_a + _swap_half(zc, 32) * sin_a

    r = [rope_a(c) for c in range(8)]
    for c in range(2):
        qa_ref[:, c * LANES:(c + 1) * LANES] = (r[c] * 0.125).astype(BF16)
        qi_ref[:, c * LANES:(c + 1) * LANES] = (r[2 + c] * 0.125).astype(BF16)
        qc_ref[:, c * LANES:(c + 1) * LANES] = r[4 + c] * 0.125
    gate_ref[...] = z[:, 1024:2048]

    cq = _rms(z[:, 2048:2304], gq_ref[...]).astype(BF16)
    qb = _dot(cq, wuq_ref[...])
    for p in range(B_HEADS // 2):
        ql = _dot(qb[:, p * LANES:(p + 1) * LANES].astype(BF16), wuk_ref[p])
        qcat_ref[2 * p, :, 0:LANES] = ql[:, 0:LANES].astype(BF16)
        qcat_ref[2 * p + 1, :, 0:LANES] = ql[:, LANES:2 * LANES].astype(BF16)
    for c in range(2):
        zc = qb[:, 512 + c * LANES:512 + (c + 1) * LANES]
        rq = zc * cos_q + _swap_half(zc, 16) * sin_q
        for j in range(4):
            sh = rq if j == 0 else pltpu.roll(rq, LANES - ROPE_DIM_B * j, 1)
            qcat_ref[4 * c + j, :, LANES:2 * LANES] = jnp.where(lane < ROPE_DIM_B, sh, 0.0).astype(BF16)

    ckv = _rms(z[:, 2304:2432], gkv_ref[...])
    misc = z[:, 2560:2688]
    misc = misc * cos_m + _swap_half(misc, 16) * sin_m
    rolled = pltpu.roll(misc, 64, 1)
    aw_ref[...] = rolled
    dkv = jnp.where(lane < 64, r[7], rolled)
    kidx = pltpu.roll(r[7], 64, 1)
    c_v = z[:, 2432:2560]
    if rows_t:
        dkv_ref[...] = dkv.T
        didx_ref[...] = kidx.T[0:IDX_DIM, :]
        mla_ref[0:KV_LORA, :] = ckv.T
        mla_ref[KV_LORA:LAT_DIM, :] = rolled.T[0:ROPE_DIM_B, :]
        moba_ref[0:LANES, :] = r[6].T
        moba_ref[LANES:2 * LANES, :] = c_v.T
    else:
        dkv_ref[...] = dkv
        didx_ref[...] = kidx[:, 0:IDX_DIM]
        mla_ref[:, 0:KV_LORA] = ckv
        mla_ref[:, KV_LORA:LAT_DIM] = rolled[:, 0:ROPE_DIM_B]
        moba_ref[:, 0:LANES] = r[6]
        moba_ref[:, LANES:2 * LANES] = c_v


def _project(x2d, tab, g, wperm, gq, wuq, wukp, gkv, *, tm, batch_t):
    n, d = x2d.shape
    steps = n // tm
    ntab = tab.shape[0] // tm
    row = lambda w: pl.BlockSpec((tm, w), lambda i: (i, 0))
    const2 = lambda a: pl.BlockSpec(a.shape, lambda i: (0, 0))
    in_specs = [
        row(d), const2(g), const2(wperm),
        pl.BlockSpec((tm, TAB_WIDTH), lambda i: (i % ntab, 0)),
        const2(gq), const2(wuq), pl.BlockSpec(wukp.shape, lambda i: (0, 0, 0)), const2(gkv),
    ]
    out_shape = [
        jax.ShapeDtypeStruct((n, 256), BF16), jax.ShapeDtypeStruct((n, 256), BF16),
        jax.ShapeDtypeStruct((n, 256), F32), jax.ShapeDtypeStruct((B_HEADS, n, 256), BF16),
        jax.ShapeDtypeStruct((n, 1024), F32), jax.ShapeDtypeStruct((n, LANES), F32),
    ]
    out_specs = [row(256), row(256), row(256), pl.BlockSpec((B_HEADS, tm, 256), lambda i: (0, i, 0)),
                 row(1024), row(LANES)]
    feats = (2 * HEAD_DIM, IDX_DIM, LAT_DIM, 4 * HEAD_DIM)
    if batch_t is not None:
        b, t = batch_t
        per = t // tm
        for f in feats:
            out_shape.append(jax.ShapeDtypeStruct((b, f, t), F32))
            out_specs.append(pl.BlockSpec((None, f, tm), lambda i: (i // per, 0, i % per)))
    else:
        for f in feats:
            out_shape.append(jax.ShapeDtypeStruct((n, f), F32))
            out_specs.append(row(f))
    return pl.pallas_call(
        functools.partial(_project_kernel, rows_t=batch_t is not None),
        grid=(steps,), in_specs=in_specs, out_specs=out_specs, out_shape=out_shape,
        compiler_params=pltpu.CompilerParams(dimension_semantics=("arbitrary",), vmem_limit_bytes=VMEM_LIMIT_BYTES),
        name="project",
    )(x2d, g, wperm, tab, gq, wuq, wukp, gkv)


def _outproj_kernel(ao_ref, bl_ref, co_ref, gate_ref, x_ref, wuv_ref, wout_ref, fg_ref, o_ref, mix_ref, *, final):
    gate = gate_ref[...]
    sg = gate * jax.nn.sigmoid(gate)
    mix_ref[:, 0:256] = (ao_ref[...] * sg[:, 0:256]).astype(BF16)
    bl = bl_ref[...].astype(BF16)
    for p in range(B_HEADS // 2):
        bo = _dot(bl[:, p * 256:(p + 1) * 256], wuv_ref[p])
        mix_ref[:, 256 + p * LANES:256 + (p + 1) * LANES] = (bo * sg[:, 256 + p * LANES:256 + (p + 1) * LANES]).astype(BF16)
    mix_ref[:, 768:1024] = (co_ref[...] * sg[:, 768:1024]).astype(BF16)
    xo = x_ref[...] + _dot(mix_ref[...], wout_ref[...])
    if final:
        xo = _rms(xo, fg_ref[...])
    o_ref[...] = xo


def _outproj(ao, bl, co, gate, x2d, wuvp, wout, fg, *, tm, final):
    n, d = x2d.shape
    row = lambda w: pl.BlockSpec((tm, w), lambda i: (i, 0))
    return pl.pallas_call(
        functools.partial(_outproj_kernel, final=final),
        grid=(n // tm,),
        in_specs=[row(256), row(1024), row(256), row(1024), row(d),
                  pl.BlockSpec(wuvp.shape, lambda i: (0, 0, 0)), pl.BlockSpec(wout.shape, lambda i: (0, 0)),
                  pl.BlockSpec(fg.shape, lambda i: (0, 0))],
        out_specs=row(d), out_shape=jax.ShapeDtypeStruct((n, d), F32),
        scratch_shapes=[pltpu.VMEM((tm, 1024), BF16)],
        compiler_params=pltpu.CompilerParams(dimension_semantics=("arbitrary",), vmem_limit_bytes=VMEM_LIMIT_BYTES),
        name="outproj",
    )(ao, bl, co, gate, x2d, wuvp, wout, fg)


def _for_bucket(i, t, tq, body):
    for j in range(t // tq):
        pl.when(i == j)(functools.partial(body, (j + 1) * tq))


def _dsa_prompt_kernel(qa_ref, qi_ref, aw_ref, kv_ref, ki_ref, o_ref, kih_ref, kh_ref, vh_ref, *, tq, t, ktop):
    i = pl.program_id(1)

    @pl.when(i == 0)
    def _():
        kv = kv_ref[...]
        k_t = kv[0:HEAD_DIM].astype(BF16)
        v_t = kv[HEAD_DIM:2 * HEAD_DIM].astype(BF16)
        ki_t = ki_ref[...].astype(BF16)
        zero = jnp.zeros((A_HEADS * HEAD_DIM, t), BF16)
        for h in range(A_HEADS):
            kih_ref[h] = zero
            kh_ref[h] = zero
            vh_ref[h] = zero
            kih_ref[h, h * HEAD_DIM:(h + 1) * HEAD_DIM, :] = ki_t
            kh_ref[h, h * HEAD_DIM:(h + 1) * HEAD_DIM, :] = k_t
            vh_ref[h, h * HEAD_DIM:(h + 1) * HEAD_DIM, :] = v_t

    def body(kl):
        row = i * tq + _row_iota((tq, kl))
        col = _lane_iota((tq, kl))
        causal = col <= row
        if kl <= ktop:
            sel = causal
        else:
            qi = qi_ref[...]
            aw = aw_ref[...]
            score = jnp.zeros((tq, kl), F32)
            for h in range(IDX_HEADS):
                rel = jnp.maximum(_dot(qi, kih_ref[h, :, 0:kl]), 0.0)
                score = score + rel * (aw[:, ROPE_DIM_B + h:ROPE_DIM_B + h + 1] * (IDX_HEADS ** -0.5))
            score = jnp.where(score == 0.0, 0.0, score)
            score = jnp.where(causal, score, NEG_INF)
            sel = _topk_mask(score, ktop) & causal
        bias = jnp.where(sel, 0.0, NEG_INF)
        qa = qa_ref[...]
        out = jnp.zeros((tq, A_HEADS * HEAD_DIM), F32)
        for h in range(A_HEADS):
            out = out + _softmax_pv([(_dot(qa, kh_ref[h, :, 0:kl]) + bias, vh_ref[h, :, 0:kl])])
        o_ref[...] = out

    _for_bucket(i, t, tq, body)


def _dsa_prompt(qa, qi, aw, dkv_t, didx_t, *, tq):
    b, _, t = dkv_t.shape
    nq = t // tq
    qrow = lambda w: pl.BlockSpec((tq, w), lambda bb, i: (bb * nq + i, 0))
    return pl.pallas_call(
        functools.partial(_dsa_prompt_kernel, tq=tq, t=t, ktop=min(DSA_TOPK, t // 4)),
        grid=(b, nq),
        in_specs=[qrow(256), qrow(256), qrow(LANES),
                  pl.BlockSpec((None, 2 * HEAD_DIM, t), lambda bb, i: (bb, 0, 0)),
                  pl.BlockSpec((None, IDX_DIM, t), lambda bb, i: (bb, 0, 0))],
        out_specs=qrow(256), out_shape=jax.ShapeDtypeStruct((b * t, 256), F32),
        scratch_shapes=[pltpu.VMEM((A_HEADS, 256, t), BF16)] * 3,
        compiler_params=pltpu.CompilerParams(dimension_semantics=("arbitrary", "arbitrary"),
                                             vmem_limit_bytes=VMEM_LIMIT_BYTES),
        name="dsa_prompt",
    )(qa, qi, aw, dkv_t, didx_t)


def _mla_prompt_kernel(qcat_ref, lat_ref, o_ref, kp_ref, *, tq, t):
    i = pl.program_id(1)

    @pl.when(i == 0)
    def _():
        kp_ref[0:LAT_DIM, :] = lat_ref[...].astype(BF16)
        kp_ref[LAT_DIM:256, :] = jnp.zeros((256 - LAT_DIM, t), BF16)

    def body(kl):
        diag = _causal_bias(tq)
        for h in range(B_HEADS):
            s = _dot(qcat_ref[h], kp_ref[:, 0:kl])
            pieces = [(s[:, kl - tq:kl] + diag, kp_ref[0:KV_LORA, kl - tq:kl])]
            if kl > tq:
                pieces.append((s[:, 0:kl - tq], kp_ref[0:KV_LORA, 0:kl - tq]))
            o_ref[:, h * KV_LORA:(h + 1) * KV_LORA] = _softmax_pv(pieces, scale=MLA_SCALE)

    _for_bucket(i, t, tq, body)


def _mla_prompt(qcat, mla_t, *, tq):
    b, _, t = mla_t.shape
    nq = t // tq
    return pl.pallas_call(
        functools.partial(_mla_prompt_kernel, tq=tq, t=t),
        grid=(b, nq),
        in_specs=[pl.BlockSpec((B_HEADS, tq, 256), lambda bb, i: (0, bb * nq + i, 0)),
                  pl.BlockSpec((None, LAT_DIM, t), lambda bb, i: (bb, 0, 0))],
        out_specs=pl.BlockSpec((tq, B_HEADS * KV_LORA), lambda bb, i: (bb * nq + i, 0)),
        out_shape=jax.ShapeDtypeStruct((b * t, B_HEADS * KV_LORA), F32),
        scratch_shapes=[pltpu.VMEM((256, t), BF16)],
        compiler_params=pltpu.CompilerParams(dimension_semantics=("arbitrary", "arbitrary"),
                                             vmem_limit_bytes=VMEM_LIMIT_BYTES),
        name="mla_prompt",
    )(qcat, mla_t)


GATE_LANES = LANES // C_HEADS


def _moba_prompt_kernel(qc_ref, rows_ref, o_ref, kh_ref, vh_ref, kmt_ref, *, tq, t, n_sel):
    i = pl.program_id(1)
    nb = t // MOBA_BLOCK

    @pl.when(i == 0)
    def _():
        rows = rows_ref[...]
        lane = _lane_iota((2 * HEAD_DIM, LANES))
        km = jnp.zeros((2 * HEAD_DIM, LANES), F32)
        for n in range(nb):
            col = jnp.sum(rows[0:2 * HEAD_DIM, n * MOBA_BLOCK:(n + 1) * MOBA_BLOCK], axis=1, keepdims=True)
            km = jnp.where(lane == n, col * (1.0 / MOBA_BLOCK), km)
        zero = jnp.zeros((C_HEADS * HEAD_DIM, t), BF16)
        for h in range(C_HEADS):
            g = h // (C_HEADS // C_KV_HEADS)
            kh_ref[h] = zero
            vh_ref[h] = zero
            kh_ref[h, h * HEAD_DIM:(h + 1) * HEAD_DIM, :] = rows[g * HEAD_DIM:(g + 1) * HEAD_DIM].astype(BF16)
            vh_ref[h, h * HEAD_DIM:(h + 1) * HEAD_DIM, :] = rows[(2 + g) * HEAD_DIM:(3 + g) * HEAD_DIM].astype(BF16)
            kmg = km[g * HEAD_DIM:(g + 1) * HEAD_DIM]
            kmt_ref[h * HEAD_DIM:(h + 1) * HEAD_DIM, :] = kmg if h == 0 else pltpu.roll(kmg, h * GATE_LANES, 1)

    def body(kl):
        q = qc_ref[...]
        qb = q.astype(BF16)
        lane = _lane_iota((tq, LANES))
        diag = _causal_bias(tq)
        n_past = kl // MOBA_BLOCK - 1
        gate = _dot(q, kmt_ref[...], precision=HIGHEST)
        out = jnp.zeros((tq, C_HEADS * HEAD_DIM), F32)
        for h in range(C_HEADS):
            past = (lane >= h * GATE_LANES) & (lane < h * GATE_LANES + n_past)
            blk_bias = jnp.where(_topn_lanes(gate, past, n_sel), 0.0, NEG_INF)
            s = _dot(qb, kh_ref[h, :, 0:kl])
            pieces = [(s[:, kl - tq:kl] + diag, vh_ref[h, :, kl - tq:kl])]
            for c in range(n_past):
                keys = slice(c * MOBA_BLOCK, (c + 1) * MOBA_BLOCK)
                pieces.append((s[:, keys] + blk_bias[:, h * GATE_LANES + c:h * GATE_LANES + c + 1], vh_ref[h, :, keys]))
            out = out + _softmax_pv(pieces)
        o_ref[...] = out

    _for_bucket(i, t, tq, body)


def _moba_prompt(qc, moba_t, *, tq):
    b, _, t = moba_t.shape
    nq = t // tq
    nb = t // MOBA_BLOCK
    assert nb <= GATE_LANES and tq == MOBA_BLOCK
    qrow = lambda w: pl.BlockSpec((tq, w), lambda bb, i: (bb * nq + i, 0))
    return pl.pallas_call(
        functools.partial(_moba_prompt_kernel, tq=tq, t=t, n_sel=min(MOBA_TOPK, nb - 1)),
        grid=(b, nq),
        in_specs=[qrow(256), pl.BlockSpec((None, 4 * HEAD_DIM, t), lambda bb, i: (bb, 0, 0))],
        out_specs=qrow(256), out_shape=jax.ShapeDtypeStruct((b * t, 256), F32),
        scratch_shapes=[pltpu.VMEM((C_HEADS, 256, t), BF16), pltpu.VMEM((C_HEADS, 256, t), BF16),
                        pltpu.VMEM((C_HEADS * HEAD_DIM, LANES), F32)],
        compiler_params=pltpu.CompilerParams(dimension_semantics=("arbitrary", "arbitrary"),
                                             vmem_limit_bytes=VMEM_LIMIT_BYTES),
        name="moba_prompt",
    )(qc, moba_t)


def _page_specs(layer, feat, group):
    return [
        pl.BlockSpec((None, None, feat, LANES),
                     functools.partial(lambda b, g, pt, j: (layer, pt[b, g * group + j], 0, 0), j=j))
        for j in range(group)
    ]


def _per_batch(shape):
    return pl.BlockSpec((None,) + tuple(shape), lambda b, g, pt: (b,) + (0,) * len(shape))


def _dsa_score_kernel(pt_ref, qi_ref, w_ref, new_ref, *rest, group):
    pages = rest[:group]
    past_ref, newsc_ref, ki_ref = rest[group:]
    g = pl.program_id(1)
    q = qi_ref[...].astype(BF16)
    w = w_ref[...] * (IDX_HEADS ** -0.5)
    nt = q.shape[0] // IDX_HEADS

    def head_sum(keys_t):
        rel = jnp.maximum(_dot(q, keys_t), 0.0) * w
        acc = rel[0:nt]
        for h in range(1, IDX_HEADS):
            acc = acc + rel[h * nt:(h + 1) * nt]
        return acc

    for j in range(group):
        ki_ref[:, j * LANES:(j + 1) * LANES] = pages[j][...].astype(BF16)
    past_ref[...] = head_sum(ki_ref[...])

    @pl.when(g == 0)
    def _():
        newsc_ref[...] = head_sum(new_ref[...].astype(BF16))


def _dsa_score(layer, page_table, idx_cache_t, qi_s, w_s, new_idx_t, *, group):
    b, n_pages = page_table.shape
    nt = qi_s.shape[1] // IDX_HEADS
    past = n_pages * LANES
    return pl.pallas_call(
        functools.partial(_dsa_score_kernel, group=group),
        grid_spec=pltpu.PrefetchScalarGridSpec(
            num_scalar_prefetch=1, grid=(b, n_pages // group),
            in_specs=[_per_batch(qi_s.shape[1:]), _per_batch(w_s.shape[1:]), _per_batch(new_idx_t.shape[1:])]
            + _page_specs(layer, IDX_DIM, group),
            out_specs=[pl.BlockSpec((None, nt, group * LANES), lambda bb, g, pt: (bb, 0, g)),
                       _per_batch((nt, LANES))],
            scratch_shapes=[pltpu.VMEM((IDX_DIM, group * LANES), BF16)]),
        out_shape=[jax.ShapeDtypeStruct((b, nt, past), F32), jax.ShapeDtypeStruct((b, nt, LANES), F32)],
        compiler_params=pltpu.CompilerParams(dimension_semantics=("arbitrary", "arbitrary"),
                                             vmem_limit_bytes=VMEM_LIMIT_BYTES),
        name="dsa_sample_score",
    )(page_table, qi_s, w_s, new_idx_t, *([idx_cache_t] * group))


def _dsa_select_kernel(past_ref, new_ref, mask_ref, *, ktop):
    rb, nt, past = past_ref.shape
    rows = rb * nt
    score = jnp.concatenate([past_ref[...].reshape(rows, past), new_ref[...].reshape(rows, LANES)], axis=1)
    idx = _lane_iota(score.shape)
    tok = _row_iota(score.shape) % nt
    valid = (idx < past) | (idx - past <= tok)
    score = jnp.where(score == 0.0, 0.0, score)
    score = jnp.where(valid, score, NEG_INF)
    sel = _topk_mask(score, ktop) & valid
    mask_ref[...] = jnp.where(sel, 1.0, 0.0).reshape(rb, nt, past + LANES)


def _dsa_select(sc_past, sc_new, *, rb):
    b, nt, past = sc_past.shape
    return pl.pallas_call(
        functools.partial(_dsa_select_kernel, ktop=min(DSA_TOPK, (past + nt) // 4)),
        grid=(b // rb,),
        in_specs=[pl.BlockSpec((rb, nt, past), lambda i: (i, 0, 0)), pl.BlockSpec((rb, nt, LANES), lambda i: (i, 0, 0))],
        out_specs=pl.BlockSpec((rb, nt, past + LANES), lambda i: (i, 0, 0)),
        out_shape=jax.ShapeDtypeStruct((b, nt, past + LANES), F32),
        compiler_params=pltpu.CompilerParams(dimension_semantics=("arbitrary",), vmem_limit_bytes=VMEM_LIMIT_BYTES),
        name="dsa_sample_select",
    )(sc_past, sc_new)


def _dsa_attn_kernel(pt_ref, q_ref, mpast_ref, mnew_ref, new_ref, *rest, group):
    pages = rest[:group]
    o_ref, k_ref, v_ref, m_ref, l_ref, acc_ref = rest[group:]
    g = pl.program_id(1)
    q = q_ref[...].astype(BF16)

    def masked(s, mask):
        allowed = jnp.concatenate([mask] * A_HEADS, axis=0) > 0.5
        return jnp.where(allowed, s, NEG_INF)

    @pl.when(g == 0)
    def _():
        m_ref[...] = jnp.full(m_ref.shape, NEG_INF, F32)
        l_ref[...] = jnp.zeros(l_ref.shape, F32)
        acc_ref[...] = jnp.zeros(acc_ref.shape, F32)
        new = new_ref[...].astype(BF16)
        _flash_update(masked(_dot(q, new[0:HEAD_DIM]), mnew_ref[...]), new[HEAD_DIM:2 * HEAD_DIM],
                      m_ref, l_ref, acc_ref)

    for j in range(group):
        page = pages[j][...].astype(BF16)
        k_ref[:, j * LANES:(j + 1) * LANES] = page[0:HEAD_DIM]
        v_ref[:, j * LANES:(j + 1) * LANES] = page[HEAD_DIM:2 * HEAD_DIM]
    _flash_update(masked(_dot(q, k_ref[...]), mpast_ref[...]), v_ref[...], m_ref, l_ref, acc_ref)

    @pl.when(g == pl.num_programs(1) - 1)
    def _():
        o_ref[...] = acc_ref[...] / l_ref[...]


def _dsa_attn(layer, page_table, kv_cache_t, qa_s, mask, new_kv_t, *, group):
    b, n_pages = page_table.shape
    rows = qa_s.shape[1]
    nt = rows // A_HEADS
    return pl.pallas_call(
        functools.partial(_dsa_attn_kernel, group=group),
        grid_spec=pltpu.PrefetchScalarGridSpec(
            num_scalar_prefetch=1, grid=(b, n_pages // group),
            in_specs=[_per_batch(qa_s.shape[1:]),
                      pl.BlockSpec((None, nt, group * LANES), lambda bb, g, pt: (bb, 0, g)),
                      pl.BlockSpec((None, nt, LANES), lambda bb, g, pt: (bb, 0, n_pages)),
                      _per_batch(new_kv_t.shape[1:])] + _page_specs(layer, 2 * HEAD_DIM, group),
            out_specs=_per_batch((rows, HEAD_DIM)),
            scratch_shapes=[pltpu.VMEM((HEAD_DIM, group * LANES), BF16), pltpu.VMEM((HEAD_DIM, group * LANES), BF16),
                            pltpu.VMEM((rows, 1), F32), pltpu.VMEM((rows, 1), F32), pltpu.VMEM((rows, HEAD_DIM), F32)]),
        out_shape=jax.ShapeDtypeStruct((b, rows, HEAD_DIM), F32),
        compiler_params=pltpu.CompilerParams(dimension_semantics=("arbitrary", "arbitrary"),
                                             vmem_limit_bytes=VMEM_LIMIT_BYTES),
        name="dsa_sample_attn",
    )(page_table, qa_s, mask, mask, new_kv_t, *([kv_cache_t] * group))


def _mla_sample_kernel(pt_ref, q_ref, new_ref, *rest, group):
    pages = rest[:group]
    o_ref, k_ref, m_ref, l_ref, acc_ref = rest[group:]
    g = pl.program_id(1)
    q = q_ref[...].astype(BF16)
    rows = q.shape[0]
    nt = rows // B_HEADS

    @pl.when(g == 0)
    def _():
        m_ref[...] = jnp.full(m_ref.shape, NEG_INF, F32)
        l_ref[...] = jnp.zeros(l_ref.shape, F32)
        acc_ref[...] = jnp.zeros(acc_ref.shape, F32)
        k_ref[LAT_DIM:256, :] = jnp.zeros((256 - LAT_DIM, group * LANES), BF16)
        k_ref[0:LAT_DIM, 0:LANES] = new_ref[...].astype(BF16)
        s = _dot(q, k_ref[:, 0:LANES]) * MLA_SCALE
        causal = _lane_iota((rows, LANES)) <= _row_iota((rows, LANES)) % nt
        _flash_update(jnp.where(causal, s, NEG_INF), k_ref[0:KV_LORA, 0:LANES], m_ref, l_ref, acc_ref)

    for j in range(group):
        k_ref[0:LAT_DIM, j * LANES:(j + 1) * LANES] = pages[j][...].astype(BF16)
    _flash_update(_dot(q, k_ref[...]) * MLA_SCALE, k_ref[0:KV_LORA, :], m_ref, l_ref, acc_ref)

    @pl.when(g == pl.num_programs(1) - 1)
    def _():
        o_ref[...] = acc_ref[...] / l_ref[...]


def _mla_sample(layer, page_table, mla_cache_t, qcat_s, new_mla_t, *, group):
    b, n_pages = page_table.shape
    rows = qcat_s.shape[1]
    return pl.pallas_call(
        functools.partial(_mla_sample_kernel, group=group),
        grid_spec=pltpu.PrefetchScalarGridSpec(
            num_scalar_prefetch=1, grid=(b, n_pages // group),
            in_specs=[_per_batch(qcat_s.shape[1:]), _per_batch(new_mla_t.shape[1:])]
            + _page_specs(layer, LAT_DIM, group),
            out_specs=_per_batch((rows, KV_LORA)),
            scratch_shapes=[pltpu.VMEM((256, group * LANES), BF16),
                            pltpu.VMEM((rows, 1), F32), pltpu.VMEM((rows, 1), F32), pltpu.VMEM((rows, KV_LORA), F32)]),
        out_shape=jax.ShapeDtypeStruct((b, rows, KV_LORA), F32),
        compiler_params=pltpu.CompilerParams(dimension_semantics=("arbitrary", "arbitrary"),
                                             vmem_limit_bytes=VMEM_LIMIT_BYTES),
        name="mla_sample",
    )(page_table, qcat_s, new_mla_t, *([mla_cache_t] * group))


def _moba_sample_kernel(pt_ref, q_ref, new_ref, *rest, group, n_past, n_sel):
    pages = rest[:group]
    o_ref, km_ref, ms_ref, ls_ref, ob_ref = rest[group:]
    g = pl.program_id(1)
    q = q_ref[...]
    qb = q.astype(BF16)
    rows = q.shape[0]
    nt = rows // C_HEADS
    half = 2 * HEAD_DIM
    lane = _lane_iota((rows, LANES))
    per_step = group // 2

    @pl.when(g == 0)
    def _():
        km_ref[...] = jnp.zeros(km_ref.shape, F32)
        ms_ref[...] = jnp.zeros(ms_ref.shape, F32)
        ls_ref[...] = jnp.zeros(ls_ref.shape, F32)

    for jj in range(per_step):
        n = g * per_step + jj
        pa = pages[2 * jj][...]
        pb = pages[2 * jj + 1][...]
        k_t = jnp.concatenate([pa[0:half], pb[0:half]], axis=1)
        v_t = jnp.concatenate([pa[half:2 * half], pb[half:2 * half]], axis=1)
        kmean = jnp.sum(k_t, axis=1, keepdims=True) * (1.0 / MOBA_BLOCK)
        km_ref[...] = jnp.where(_lane_iota((half, LANES)) == n, kmean, km_ref[...])
        s = _dot(qb, k_t.astype(BF16))
        m = jnp.max(s, axis=-1, keepdims=True)
        p = jnp.exp(s - m)
        ms_ref[...] = jnp.where(lane == n, m, ms_ref[...])
        ls_ref[...] = jnp.where(lane == n, jnp.sum(p, axis=-1, keepdims=True), ls_ref[...])
        ob_ref[n] = _dot_nt(p.astype(BF16), v_t.astype(BF16))

    @pl.when(g == pl.num_programs(1) - 1)
    def _():
        gate = _dot(q, km_ref[...], precision=HIGHEST)
        sel = _topn_lanes(gate, lane < n_past, n_sel)
        new = new_ref[...].astype(BF16)
        s_own = jnp.where(lane <= _row_iota((rows, LANES)) % nt, _dot(qb, new[0:half]), NEG_INF)
        m_own = jnp.max(s_own, axis=-1, keepdims=True)
        m_sel = jnp.where(sel, ms_ref[...], NEG_INF)
        m_tot = jnp.maximum(jnp.max(m_sel, axis=-1, keepdims=True), m_own)
        p_own = jnp.exp(s_own - m_tot)
        wts = jnp.exp(m_sel - m_tot)
        l_tot = jnp.sum(wts * ls_ref[...], axis=-1, keepdims=True) + jnp.sum(p_own, axis=-1, keepdims=True)
        acc = _dot_nt(p_own.astype(BF16), new[half:2 * half])
        for n in range(n_past):
            acc = acc + wts[:, n:n + 1] * ob_ref[n]
        out = acc / l_tot
        upper = _row_iota((rows, LANES)) >= (C_HEADS // C_KV_HEADS) * nt
        o_ref[...] = jnp.where(upper, pltpu.roll(out, HEAD_DIM, 1), out)


def _moba_sample(layer, page_table, moba_cache_t, q2_s, new_moba_t, *, group):
    b, n_pages = page_table.shape
    rows = q2_s.shape[1]
    n_past = n_pages * LANES // MOBA_BLOCK
    return pl.pallas_call(
        functools.partial(_moba_sample_kernel, group=group, n_past=n_past, n_sel=min(MOBA_TOPK, n_past)),
        grid_spec=pltpu.PrefetchScalarGridSpec(
            num_scalar_prefetch=1, grid=(b, n_pages // group),
            in_specs=[_per_batch(q2_s.shape[1:]), _per_batch(new_moba_t.shape[1:])]
            + _page_specs(layer, 4 * HEAD_DIM, group),
            out_specs=_per_batch((rows, LANES)),
            scratch_shapes=[pltpu.VMEM((2 * HEAD_DIM, LANES), F32), pltpu.VMEM((rows, LANES), F32),
                            pltpu.VMEM((rows, LANES), F32), pltpu.VMEM((n_past, rows, LANES), F32)]),
        out_shape=jax.ShapeDtypeStruct((b, rows, LANES), F32),
        compiler_params=pltpu.CompilerParams(dimension_semantics=("arbitrary", "arbitrary"),
                                             vmem_limit_bytes=VMEM_LIMIT_BYTES),
        name="moba_sample",
    )(page_table, q2_s, new_moba_t, *([moba_cache_t] * group))


def _rope_tables(pos):
    posf = pos.astype(F32)[:, None]

    def cos_sin(half):
        inv_freq = ROPE_THETA ** (-jnp.arange(half, dtype=F32) / half)
        ang = posf * inv_freq[None, :]
        return jnp.cos(ang), jnp.sin(ang)

    c32, s32 = cos_sin(HEAD_DIM // 2)
    c16, s16 = cos_sin(ROPE_DIM_B // 2)
    cos_a = jnp.tile(c32, (1, 4))
    sin_a = jnp.tile(jnp.concatenate([-s32, s32], axis=1), (1, 2))
    cos_q = jnp.tile(c16, (1, 8))
    sin_q = jnp.tile(jnp.concatenate([-s16, s16], axis=1), (1, 4))
    lane = jnp.arange(LANES)[None, :]
    in_kr = (lane >= 64) & (lane < 64 + ROPE_DIM_B)
    cos_m = jnp.where(in_kr, cos_q, 1.0)
    sin_m = jnp.where(in_kr, sin_q, 0.0)
    return jnp.concatenate([cos_a, sin_a, cos_m, sin_m, cos_q, sin_q], axis=1)


def _layer_weights(layer, norm_g, w_in, w_out, mla_g_q, mla_w_uq, mla_g_kv, mla_w_uk, mla_w_uv):
    w = w_in[layer]
    cols = [w[:, _OFF[n][0]:_OFF[n][1]] for n in _PERM]
    used = sum(c.shape[1] for c in cols)
    cols.append(jnp.zeros((w.shape[0], PROJ_WIDTH - used), w.dtype))
    wperm = jnp.concatenate(cols, axis=1).astype(BF16)
    uq = mla_w_uq[layer]
    wuq = jnp.concatenate([uq[:, :, :NOPE_DIM].reshape(Q_LORA, -1), uq[:, :, NOPE_DIM:].reshape(Q_LORA, -1)],
                          axis=1).astype(BF16)
    uk = jnp.transpose(mla_w_uk[layer], (1, 2, 0))
    uv = jnp.transpose(mla_w_uv[layer], (1, 0, 2))
    zk = jnp.zeros((NOPE_DIM, KV_LORA), F32)
    zv = jnp.zeros((KV_LORA, V_DIM_B), F32)
    wukp = jnp.stack([jnp.block([[uk[2 * p], zk], [zk, uk[2 * p + 1]]]) for p in range(B_HEADS // 2)]).astype(BF16)
    wuvp = jnp.stack([jnp.block([[uv[2 * p], zv], [zv, uv[2 * p + 1]]]) for p in range(B_HEADS // 2)]).astype(BF16)
    return dict(g=norm_g[layer][None, :], wperm=wperm, gq=mla_g_q[layer][None, :], wuq=wuq, wukp=wukp,
                gkv=mla_g_kv[layer][None, :], wuvp=wuvp, wout=w_out[layer].astype(BF16))


def _heads_first(a, b, nt, heads):
    d = a.shape[-1] // heads
    return a.reshape(b, nt, heads, d).transpose(0, 2, 1, 3).reshape(b, heads * nt, d)


def _tokens_first(a, b, nt, heads):
    d = a.shape[-1]
    return a.reshape(b, heads, nt, d).transpose(0, 2, 1, 3).reshape(b * nt, heads * d)


def _new_page(rows, b, nt):
    f = rows.shape[-1]
    return jnp.pad(rows.reshape(b, nt, f).transpose(0, 2, 1), ((0, 0), (0, 0), (0, LANES - nt)))


def kernel(x_prompt, x_sample, cache_dsa_kv, cache_dsa_idx, cache_mla, cache_moba_kv, page_table, norm_g, w_in, w_out, mla_g_q, mla_w_uq, mla_g_kv, mla_w_uk, mla_w_uv, final_g):
    depth = w_in.shape[0]
    bp, t, d = x_prompt.shape
    bs, nt, _ = x_sample.shape
    n_phys, page = cache_mla.shape[1], cache_mla.shape[2]
    n_pages = page_table.shape[1]
    past = n_pages * page
    assert page == LANES and t % MOBA_BLOCK == 0 and past % MOBA_BLOCK == 0 and nt == 8
    tm_p = 512
    tm_s = min(512, bs * nt)
    tq = 256
    group = min(64, n_pages)
    rb = min(8, bs)

    tab_p = _rope_tables(jnp.arange(t, dtype=jnp.int32))
    tab_s = jnp.tile(_rope_tables(past + jnp.arange(nt, dtype=jnp.int32)), (tm_s // nt, 1))
    fg = final_g[None, :]

    idx_t = jnp.transpose(cache_dsa_idx, (0, 1, 3, 2))
    kv_t = jnp.transpose(cache_dsa_kv, (0, 1, 3, 4, 2)).reshape(depth, n_phys, 2 * HEAD_DIM, page)
    mla_t = jnp.transpose(cache_mla, (0, 1, 3, 2))
    moba_t = jnp.transpose(cache_moba_kv, (0, 1, 3, 4, 5, 2)).reshape(depth, n_phys, 4 * HEAD_DIM, page)

    xp = x_prompt.reshape(bp * t, d)
    xs = x_sample.reshape(bs * nt, d)
    rows_p, rows_s = [], []
    for layer in range(depth):
        wl = _layer_weights(layer, norm_g, w_in, w_out, mla_g_q, mla_w_uq, mla_g_kv, mla_w_uk, mla_w_uv)
        final = layer == depth - 1

        qa, qi, qc, qcat, gate, aw, dkv_t, didx_t, lat_t, mob_t = _project(
            xp, tab_p, wl["g"], wl["wperm"], wl["gq"], wl["wuq"], wl["wukp"], wl["gkv"], tm=tm_p, batch_t=(bp, t))
        a_o = _dsa_prompt(qa, qi, aw, dkv_t, didx_t, tq=tq)
        b_lat = _mla_prompt(qcat, lat_t, tq=tq)
        c_o = _moba_prompt(qc, mob_t, tq=tq)
        xp = _outproj(a_o, b_lat, c_o, gate, xp, wl["wuvp"], wl["wout"], fg, tm=tm_p, final=final)
        rows_p.append((dkv_t, didx_t, lat_t, mob_t))

        qa, qi, qc, qcat, gate, aw, dkv, didx, lat, mob = _project(
            xs, tab_s, wl["g"], wl["wperm"], wl["gq"], wl["wuq"], wl["wukp"], wl["gkv"], tm=tm_s, batch_t=None)
        qa_s = _heads_first(qa.astype(F32), bs, nt, A_HEADS)
        qi_s = _heads_first(qi.astype(F32), bs, nt, IDX_HEADS)
        w_s = _heads_first(aw[:, ROPE_DIM_B:ROPE_DIM_B + IDX_HEADS], bs, nt, IDX_HEADS)
        qcat_s = qcat.astype(F32).reshape(B_HEADS, bs, nt, 256).transpose(1, 0, 2, 3).reshape(bs, B_HEADS * nt, 256)
        qc_h = _heads_first(qc, bs, nt, C_HEADS).reshape(bs, C_KV_HEADS, (C_HEADS // C_KV_HEADS) * nt, HEAD_DIM)
        zq = jnp.zeros_like(qc_h[:, 0])
        q2_s = jnp.concatenate([jnp.concatenate([qc_h[:, 0], zq], axis=-1),
                                jnp.concatenate([zq, qc_h[:, 1]], axis=-1)], axis=1)
        sc_past, sc_new = _dsa_score(layer, page_table, idx_t, qi_s, w_s, _new_page(didx, bs, nt), group=n_pages)
        mask = _dsa_select(sc_past, sc_new, rb=rb)
        a_o = _dsa_attn(layer, page_table, kv_t, qa_s, mask, _new_page(dkv, bs, nt), group=group)
        b_lat = _mla_sample(layer, page_table, mla_t, qcat_s, _new_page(lat, bs, nt), group=group)
        c_o = _moba_sample(layer, page_table, moba_t, q2_s, _new_page(mob, bs, nt), group=group)
        xs = _outproj(_tokens_first(a_o, bs, nt, A_HEADS), _tokens_first(b_lat, bs, nt, B_HEADS),
                      _tokens_first(c_o[:, :, :HEAD_DIM], bs, nt, C_HEADS), gate, xs,
                      wl["wuvp"], wl["wout"], fg, tm=tm_s, final=final)
        rows_s.append((dkv, didx, lat, mob))

    def prompt_rows(k, feat_shape):
        a = jnp.stack([r[k] for r in rows_p])
        a = a.reshape((depth, bp) + feat_shape + (t,))
        nd = a.ndim
        return jnp.transpose(a, (0, 1, nd - 1) + tuple(range(2, nd - 1)))

    def sample_rows(k, feat_shape):
        return jnp.stack([r[k] for r in rows_s]).reshape((depth, bs, nt) + feat_shape)

    shapes = ((2, HEAD_DIM), (IDX_DIM,), (LAT_DIM,), (2, C_KV_HEADS, HEAD_DIM))
    return (xp.reshape(bp, t, d), xs.reshape(bs, nt, d),
            *[prompt_rows(k, s) for k, s in enumerate(shapes)],
            *[sample_rows(k, s) for k, s in enumerate(shapes)])
```

```python
import functools

import jax
import jax.numpy as jnp
from jax import lax
from jax.experimental import pallas as pl
from jax.experimental.pallas import tpu as pltpu

F32 = jnp.float32
BF16 = jnp.bfloat16
NEG_INF = float("-inf")
HIGHEST = lax.Precision.HIGHEST

HEAD_DIM = 64
ROPE_THETA = 10000.0
NORM_EPS = 1e-6
A_HEADS = 4
IDX_HEADS = 4
IDX_DIM = 64
DSA_TOPK = 256
B_HEADS = 8
Q_LORA = 256
KV_LORA = 128
NOPE_DIM = 64
ROPE_DIM_B = 32
V_DIM_B = 64
MLA_SCALE = (NOPE_DIM + ROPE_DIM_B) ** -0.5
C_HEADS = 4
C_KV_HEADS = 2
MOBA_BLOCK = 256
MOBA_TOPK = 3
LAT_DIM = KV_LORA + ROPE_DIM_B

LANES = 128
VMEM_LIMIT_BYTES = 56 * 1024 * 1024

_SPLITS = (
    ("a_q", 256), ("a_k", 64), ("a_v", 64), ("a_qi", 256), ("a_ki", 64), ("a_w", 4), ("a_gate", 256),
    ("b_cq", 256), ("b_ckv", 128), ("b_kr", 32), ("b_gate", 512),
    ("c_q", 256), ("c_k", 128), ("c_v", 128), ("c_gate", 256),
)
_OFF = {}
_o = 0
for _n, _w in _SPLITS:
    _OFF[_n] = (_o, _o + _w)
    _o += _w
IN_WIDTH = _o
_PERM = ("a_q", "a_qi", "c_q", "c_k", "a_k", "a_ki", "a_gate", "b_gate", "c_gate", "b_cq", "b_ckv", "c_v",
         "a_v", "b_kr", "a_w")
PROJ_WIDTH = 2688
TAB_WIDTH = 6 * LANES


def _dot(a, b, precision=None):
    return jnp.dot(a, b, preferred_element_type=F32, precision=precision)


def _dot_nt(a, b):
    return lax.dot_general(a, b, (((1,), (1,)), ((), ())), preferred_element_type=F32)


def _lane_iota(shape):
    return lax.broadcasted_iota(jnp.int32, shape, len(shape) - 1)


def _row_iota(shape):
    return lax.broadcasted_iota(jnp.int32, shape, len(shape) - 2)


def _swap_half(x, half):
    n = x.shape[-1]
    lane = _lane_iota(x.shape)
    fwd = pltpu.roll(x, n - half, 1)
    bwd = pltpu.roll(x, half, 1)
    return jnp.where((lane & (2 * half - 1)) < half, fwd, bwd)


def _rms(x, g):
    return x * lax.rsqrt(jnp.mean(x * x, axis=-1, keepdims=True) + NORM_EPS) * g


def _topk_mask(score, k):
    rows, width = score.shape
    bits = lax.bitcast_convert_type(score, jnp.int32)
    u = jnp.where(bits < 0, bits ^ jnp.int32(0x7FFFFFFF), bits)
    kf = float(k)

    def count_ge(t):
        return jnp.sum(jnp.where(u >= t, 1.0, 0.0), axis=-1, keepdims=True)

    zero = jnp.zeros((rows, 1), jnp.int32)
    base = jnp.where(count_ge(zero) >= kf, zero, jnp.int32(-2 ** 31))

    def value_step(it, base):
        cand = base | jnp.left_shift(jnp.int32(1), 30 - it)
        return jnp.where(count_ge(cand) >= kf, cand, base)

    thr = lax.fori_loop(0, 31, value_step, base)
    gt = u > thr
    eq = u == thr
    need = kf - jnp.sum(jnp.where(gt, 1.0, 0.0), axis=-1, keepdims=True)
    idx = _lane_iota(score.shape)
    nbits = int(width).bit_length()

    def index_step(it, lim):
        cand = lim | jnp.left_shift(jnp.int32(1), nbits - 1 - it)
        below = jnp.sum(jnp.where(eq & (idx < cand), 1.0, 0.0), axis=-1, keepdims=True)
        return jnp.where(below < need, cand, lim)

    n_eq = jnp.sum(jnp.where(eq, 1.0, 0.0), axis=-1, keepdims=True)
    tied = jnp.max(jnp.where(n_eq != need, 1.0, 0.0)) > 0.5
    lim = lax.cond(tied,
                   lambda: lax.fori_loop(0, nbits, index_step, zero),
                   lambda: jnp.full((rows, 1), 2 ** nbits - 1, jnp.int32))
    return gt | (eq & (idx <= lim))


def _topn_lanes(gate, allowed, n_sel):
    lanef = _lane_iota(gate.shape).astype(F32)
    gm = jnp.where(allowed, gate, NEG_INF)
    sel = jnp.zeros(gate.shape, jnp.bool_)
    for _ in range(n_sel):
        m = jnp.max(gm, axis=-1, keepdims=True)
        first = jnp.min(jnp.where(gm == m, lanef, float(LANES)), axis=-1, keepdims=True)
        pick = lanef == first
        sel = sel | pick
        gm = jnp.where(pick, NEG_INF, gm)
    return sel & allowed


LOG2_E = 1.4426950408889634


def _softmax_pv(pieces, scale=1.0):
    m = None
    for s, _ in pieces:
        mj = jnp.max(s, axis=-1, keepdims=True)
        m = mj if m is None else jnp.maximum(m, mj)
    l = None
    o = None
    for s, v_t in pieces:
        p = jnp.exp2((s - m) * (scale * LOG2_E))
        lj = jnp.sum(p, axis=-1, keepdims=True)
        oj = _dot_nt(p.astype(BF16), v_t)
        l = lj if l is None else l + lj
        o = oj if o is None else o + oj
    return o / l


def _causal_bias(tq):
    return jnp.where(_lane_iota((tq, tq)) <= _row_iota((tq, tq)), 0.0, NEG_INF)


def _flash_update(s, v_t, m_ref, l_ref, acc_ref, scale=1.0):
    m_old = m_ref[...]
    m_new = jnp.maximum(m_old, jnp.max(s, axis=-1, keepdims=True))
    m_safe = jnp.where(m_new == NEG_INF, 0.0, m_new)
    alpha = jnp.exp2((m_old - m_safe) * (scale * LOG2_E))
    p = jnp.exp2((s - m_safe) * (scale * LOG2_E))
    l_ref[...] = alpha * l_ref[...] + jnp.sum(p, axis=-1, keepdims=True)
    acc_ref[...] = alpha * acc_ref[...] + _dot_nt(p.astype(BF16), v_t)
    m_ref[...] = m_new


def _project_kernel(x_ref, g_ref, w_ref, tab_ref, gq_ref, wuq_ref, wuk_ref, gkv_ref,
                    qa_ref, qi_ref, qc_ref, qcat_ref, gate_ref, aw_ref,
                    dkv_ref, didx_ref, mla_ref, moba_ref, *, rows_t):
    h = _rms(x_ref[...], g_ref[...]).astype(BF16)
    z = _dot(h, w_ref[...])
    tab = tab_ref[...]
    cos_a, sin_a, cos_m, sin_m, cos_q, sin_q = [tab[:, i * LANES:(i + 1) * LANES] for i in range(6)]
    lane = _lane_iota(cos_a.shape)

    def rope_a(c):
        zc = z[:, c * LANES:(c + 1) * LANES]
        return zc * cos_a + _swap_half(zc, 32) * sin_a

    r = [rope_a(c) for c in range(8)]
    for c in range(2):
        qa_ref[:, c * LANES:(c + 1) * LANES] = (r[c] * 0.125).astype(BF16)
        qi_ref[:, c * LANES:(c + 1) * LANES] = (r[2 + c] * 0.125).astype(BF16)
        qc_ref[:, c * LANES:(c + 1) * LANES] = r[4 + c] * 0.125
    gate_ref[...] = z[:, 1024:2048]

    cq = _rms(z[:, 2048:2304], gq_ref[...]).astype(BF16)
    qb = _dot(cq, wuq_ref[...])
    for p in range(B_HEADS // 2):
        ql = _dot(qb[:, p * LANES:(p + 1) * LANES].astype(BF16), wuk_ref[p])
        qcat_ref[2 * p, :, 0:LANES] = ql[:, 0:LANES].astype(BF16)
        qcat_ref[2 * p + 1, :, 0:LANES] = ql[:, LANES:2 * LANES].astype(BF16)
    for c in range(2):
        zc = qb[:, 512 + c * LANES:512 + (c + 1) * LANES]
        rq = zc * cos_q + _swap_half(zc, 16) * sin_q
        for j in range(4):
            sh = rq if j == 0 else pltpu.roll(rq, LANES - ROPE_DIM_B * j, 1)
            qcat_ref[4 * c + j, :, LANES:2 * LANES] = jnp.where(lane < ROPE_DIM_B, sh, 0.0).astype(BF16)

    ckv = _rms(z[:, 2304:2432], gkv_ref[...])
    misc = z[:, 2560:2688]
    misc = misc * cos_m + _swap_half(misc, 16) * sin_m
    rolled = pltpu.roll(misc, 64, 1)
    aw_ref[...] = rolled
    dkv = jnp.where(lane < 64, r[7], rolled)
    kidx = pltpu.roll(r[7], 64, 1)
    c_v = z[:, 2432:2560]
    if rows_t:
        dkv_ref[...] = dkv.T
        didx_ref[...] = kidx.T[0:IDX_DIM, :]
        mla_ref[0:KV_LORA, :] = ckv.T
        mla_ref[KV_LORA:LAT_DIM, :] = rolled.T[0:ROPE_DIM_B, :]
        moba_ref[0:LANES, :] = r[6].T
        moba_ref[LANES:2 * LANES, :] = c_v.T
    else:
        dkv_ref[...] = dkv
        didx_ref[...] = kidx[:, 0:IDX_DIM]
        mla_ref[:, 0:KV_LORA] = ckv
        mla_ref[:, KV_LORA:LAT_DIM] = rolled[:, 0:ROPE_DIM_B]
        moba_ref[:, 0:LANES] = r[6]
        moba_ref[:, LANES:2 * LANES] = c_v


def _project(x2d, tab, g, wperm, gq, wuq, wukp, gkv, *, tm, batch_t):
    n, d = x2d.shape
    steps = n // tm
    ntab = tab.shape[0] // tm
    row = lambda w: pl.BlockSpec((tm, w), lambda i: (i, 0))
    const2 = lambda a: pl.BlockSpec(a.shape, lambda i: (0, 0))
    in_specs = [
        row(d), const2(g), const2(wperm),
        pl.BlockSpec((tm, TAB_WIDTH), lambda i: (i % ntab, 0)),
        const2(gq), const2(wuq), pl.BlockSpec(wukp.shape, lambda i: (0, 0, 0)), const2(gkv),
    ]
    out_shape = [
        jax.ShapeDtypeStruct((n, 256), BF16), jax.ShapeDtypeStruct((n, 256), BF16),
        jax.ShapeDtypeStruct((n, 256), F32), jax.ShapeDtypeStruct((B_HEADS, n, 256), BF16),
        jax.ShapeDtypeStruct((n, 1024), F32), jax.ShapeDtypeStruct((n, LANES), F32),
    ]
    out_specs = [row(256), row(256), row(256), pl.BlockSpec((B_HEADS, tm, 256), lambda i: (0, i, 0)),
                 row(1024), row(LANES)]
    feats = (2 * HEAD_DIM, IDX_DIM, LAT_DIM, 4 * HEAD_DIM)
    if batch_t is not None:
        b, t = batch_t
        per = t // tm
        for f in feats:
            out_shape.append(jax.ShapeDtypeStruct((b, f, t), F32))
            out_specs.append(pl.BlockSpec((None, f, tm), lambda i: (i // per, 0, i % per)))
    else:
        for f in feats:
            out_shape.append(jax.ShapeDtypeStruct((n, f), F32))
            out_specs.append(row(f))
    return pl.pallas_call(
        functools.partial(_project_kernel, rows_t=batch_t is not None),
        grid=(steps,), in_specs=in_specs, out_specs=out_specs, out_shape=out_shape,
        compiler_params=pltpu.CompilerParams(dimension_semantics=("arbitrary",), vmem_limit_bytes=VMEM_LIMIT_BYTES),
        name="project",
    )(x2d, g, wperm, tab, gq, wuq, wukp, gkv)


def _outproj_kernel(ao_ref, bl_ref, co_ref, gate_ref, x_ref, wuv_ref, wout_ref, fg_ref, o_ref, mix_ref, *, final):
    gate = gate_ref[...]
    sg = gate * jax.nn.sigmoid(gate)
    mix_ref[:, 0:256] = (ao_ref[...] * sg[:, 0:256]).astype(BF16)
    bl = bl_ref[...].astype(BF16)
    for p in range(B_HEADS // 2):
        bo = _dot(bl[:, p * 256:(p + 1) * 256], wuv_ref[p])
        mix_ref[:, 256 + p * LANES:256 + (p + 1) * LANES] = (bo * sg[:, 256 + p * LANES:256 + (p + 1) * LANES]).astype(BF16)
    mix_ref[:, 768:1024] = (co_ref[...] * sg[:, 768:1024]).astype(BF16)
    xo = x_ref[...] + _dot(mix_ref[...], wout_ref[...])
    if final:
        xo = _rms(xo, fg_ref[...])
    o_ref[...] = xo


def _outproj(ao, bl, co, gate, x2d, wuvp, wout, fg, *, tm, final):
    n, d = x2d.shape
    row = lambda w: pl.BlockSpec((tm, w), lambda i: (i, 0))
    return pl.pallas_call(
        functools.partial(_outproj_kernel, final=final),
        grid=(n // tm,),
        in_specs=[row(256), row(1024), row(256), row(1024), row(d),
                  pl.BlockSpec(wuvp.shape, lambda i: (0, 0, 0)), pl.BlockSpec(wout.shape, lambda i: (0, 0)),
                  pl.BlockSpec(fg.shape, lambda i: (0, 0))],
        out_specs=row(d), out_shape=jax.ShapeDtypeStruct((n, d), F32),
        scratch_shapes=[pltpu.VMEM((tm, 1024), BF16)],
        compiler_params=pltpu.CompilerParams(dimension_semantics=("arbitrary",), vmem_limit_bytes=VMEM_LIMIT_BYTES),
        name="outproj",
    )(ao, bl, co, gate, x2d, wuvp, wout, fg)


def _for_bucket(i, t, tq, body):
    for j in range(t // tq):
        pl.when(i == j)(functools.partial(body, (j + 1) * tq))


def _dsa_prompt_kernel(qa_ref, qi_ref, aw_ref, kv_ref, ki_ref, o_ref, kih_ref, kh_ref, vh_ref, *, tq, t, ktop):
    i = pl.program_id(1)

    @pl.when(i == 0)
    def _():
        kv = kv_ref[...]
        k_t = kv[0:HEAD_DIM].astype(BF16)
        v_t = kv[HEAD_DIM:2 * HEAD_DIM].astype(BF16)
        ki_t = ki_ref[...].astype(BF16)
        zero = jnp.zeros((A_HEADS * HEAD_DIM, t), BF16)
        for h in range(A_HEADS):
            kih_ref[h] = zero
            kh_ref[h] = zero
            vh_ref[h] = zero
            kih_ref[h, h * HEAD_DIM:(h + 1) * HEAD_DIM, :] = ki_t
            kh_ref[h, h * HEAD_DIM:(h + 1) * HEAD_DIM, :] = k_t
            vh_ref[h, h * HEAD_DIM:(h + 1) * HEAD_DIM, :] = v_t

    def body(kl):
        row = i * tq + _row_iota((tq, kl))
        col = _lane_iota((tq, kl))
        causal = col <= row
        if kl <= ktop:
            sel = causal
        else:
            qi = qi_ref[...]
            aw = aw_ref[...]
            score = jnp.zeros((tq, kl), F32)
            for h in range(IDX_HEADS):
                rel = jnp.maximum(_dot(qi, kih_ref[h, :, 0:kl]), 0.0)
                score = score + rel * (aw[:, ROPE_DIM_B + h:ROPE_DIM_B + h + 1] * (IDX_HEADS ** -0.5))
            score = jnp.where(score == 0.0, 0.0, score)
            score = jnp.where(causal, score, NEG_INF)
            sel = _topk_mask(score, ktop) & causal
        bias = jnp.where(sel, 0.0, NEG_INF)
        qa = qa_ref[...]
        out = jnp.zeros((tq, A_HEADS * HEAD_DIM), F32)
        for h in range(A_HEADS):
            out = out + _softmax_pv([(_dot(qa, kh_ref[h, :, 0:kl]) + bias, vh_ref[h, :, 0:kl])])
        o_ref[...] = out

    _for_bucket(i, t, tq, body)


def _dsa_prompt(qa, qi, aw, dkv_t, didx_t, *, tq):
    b, _, t = dkv_t.shape
    nq = t // tq
    qrow = lambda w: pl.BlockSpec((tq, w), lambda bb, i: (bb * nq + i, 0))
    return pl.pallas_call(
        functools.partial(_dsa_prompt_kernel, tq=tq, t=t, ktop=min(DSA_TOPK, t // 4)),
        grid=(b, nq),
        in_specs=[qrow(256), qrow(256), qrow(LANES),
                  pl.BlockSpec((None, 2 * HEAD_DIM, t), lambda bb, i: (bb, 0, 0)),
                  pl.BlockSpec((None, IDX_DIM, t), lambda bb, i: (bb, 0, 0))],
        out_specs=qrow(256), out_shape=jax.ShapeDtypeStruct((b * t, 256), F32),
        scratch_shapes=[pltpu.VMEM((A_HEADS, 256, t), BF16)] * 3,
        compiler_params=pltpu.CompilerParams(dimension_semantics=("arbitrary", "arbitrary"),
                                             vmem_limit_bytes=VMEM_LIMIT_BYTES),
        name="dsa_prompt",
    )(qa, qi, aw, dkv_t, didx_t)


def _mla_prompt_kernel(qcat_ref, lat_ref, o_ref, kp_ref, *, tq, t):
    i = pl.program_id(1)

    @pl.when(i == 0)
    def _():
        kp_ref[0:LAT_DIM, :] = lat_ref[...].astype(BF16)
        kp_ref[LAT_DIM:256, :] = jnp.zeros((256 - LAT_DIM, t), BF16)

    def body(kl):
        diag = _causal_bias(tq)
        for h in range(B_HEADS):
            s = _dot(qcat_ref[h], kp_ref[:, 0:kl])
            pieces = [(s[:, kl - tq:kl] + diag, kp_ref[0:KV_LORA, kl - tq:kl])]
            if kl > tq:
                pieces.append((s[:, 0:kl - tq], kp_ref[0:KV_LORA, 0:kl - tq]))
            o_ref[:, h * KV_LORA:(h + 1) * KV_LORA] = _softmax_pv(pieces, scale=MLA_SCALE)

    _for_bucket(i, t, tq, body)


def _mla_prompt(qcat, mla_t, *, tq):
    b, _, t = mla_t.shape
    nq = t // tq
    return pl.pallas_call(
        functools.partial(_mla_prompt_kernel, tq=tq, t=t),
        grid=(b, nq),
        in_specs=[pl.BlockSpec((B_HEADS, tq, 256), lambda bb, i: (0, bb * nq + i, 0)),
                  pl.BlockSpec((None, LAT_DIM, t), lambda bb, i: (bb, 0, 0))],
        out_specs=pl.BlockSpec((tq, B_HEADS * KV_LORA), lambda bb, i: (bb * nq + i, 0)),
        out_shape=jax.ShapeDtypeStruct((b * t, B_HEADS * KV_LORA), F32),
        scratch_shapes=[pltpu.VMEM((256, t), BF16)],
        compiler_params=pltpu.CompilerParams(dimension_semantics=("arbitrary", "arbitrary"),
                                             vmem_limit_bytes=VMEM_LIMIT_BYTES),
        name="mla_prompt",
    )(qcat, mla_t)


GATE_LANES = LANES // C_HEADS


def _moba_prompt_kernel(qc_ref, rows_ref, o_ref, kh_ref, vh_ref, kmt_ref, *, tq, t, n_sel):
    i = pl.program_id(1)
    nb = t // MOBA_BLOCK

    @pl.when(i == 0)
    def _():
        rows = rows_ref[...]
        lane = _lane_iota((2 * HEAD_DIM, LANES))
        km = jnp.zeros((2 * HEAD_DIM, LANES), F32)
        for n in range(nb):
            col = jnp.sum(rows[0:2 * HEAD_DIM, n * MOBA_BLOCK:(n + 1) * MOBA_BLOCK], axis=1, keepdims=True)
            km = jnp.where(lane == n, col * (1.0 / MOBA_BLOCK), km)
        zero = jnp.zeros((C_HEADS * HEAD_DIM, t), BF16)
        for h in range(C_HEADS):
            g = h // (C_HEADS // C_KV_HEADS)
            kh_ref[h] = zero
            vh_ref[h] = zero
            kh_ref[h, h * HEAD_DIM:(h + 1) * HEAD_DIM, :] = rows[g * HEAD_DIM:(g + 1) * HEAD_DIM].astype(BF16)
            vh_ref[h, h * HEAD_DIM:(h + 1) * HEAD_DIM, :] = rows[(2 + g) * HEAD_DIM:(3 + g) * HEAD_DIM].astype(BF16)
            kmg = km[g * HEAD_DIM:(g + 1) * HEAD_DIM]
            kmt_ref[h * HEAD_DIM:(h + 1) * HEAD_DIM, :] = kmg if h == 0 else pltpu.roll(kmg, h * GATE_LANES, 1)

    def body(kl):
        q = qc_ref[...]
        qb = q.astype(BF16)
        lane = _lane_iota((tq, LANES))
        diag = _causal_bias(tq)
        n_past = kl // MOBA_BLOCK - 1
        gate = _dot(q, kmt_ref[...], precision=HIGHEST)
        out = jnp.zeros((tq, C_HEADS * HEAD_DIM), F32)
        for h in range(C_HEADS):
            past = (lane >= h * GATE_LANES) & (lane < h * GATE_LANES + n_past)
            blk_bias = jnp.where(_topn_lanes(gate, past, n_sel), 0.0, NEG_INF)
            s = _dot(qb, kh_ref[h, :, 0:kl])
            pieces = [(s[:, kl - tq:kl] + diag, vh_ref[h, :, kl - tq:kl])]
            for c in range(n_past):
                keys = slice(c * MOBA_BLOCK, (c + 1) * MOBA_BLOCK)
                pieces.append((s[:, keys] + blk_bias[:, h * GATE_LANES + c:h * GATE_LANES + c + 1], vh_ref[h, :, keys]))
            out = out + _softmax_pv(pieces)
        o_ref[...] = out

    _for_bucket(i, t, tq, body)


def _moba_prompt(qc, moba_t, *, tq):
    b, _, t = moba_t.shape
    nq = t // tq
    nb = t // MOBA_BLOCK
    assert nb <= GATE_LANES and tq == MOBA_BLOCK
    qrow = lambda w: pl.BlockSpec((tq, w), lambda bb, i: (bb * nq + i, 0))
    return pl.pallas_call(
        functools.partial(_moba_prompt_kernel, tq=tq, t=t, n_sel=min(MOBA_TOPK, nb - 1)),
        grid=(b, nq),
        in_specs=[qrow(256), pl.BlockSpec((None, 4 * HEAD_DIM, t), lambda bb, i: (bb, 0, 0))],
        out_specs=qrow(256), out_shape=jax.ShapeDtypeStruct((b * t, 256), F32),
        scratch_shapes=[pltpu.VMEM((C_HEADS, 256, t), BF16), pltpu.VMEM((C_HEADS, 256, t), BF16),
                        pltpu.VMEM((C_HEADS * HEAD_DIM, LANES), F32)],
        compiler_params=pltpu.CompilerParams(dimension_semantics=("arbitrary", "arbitrary"),
                                             vmem_limit_bytes=VMEM_LIMIT_BYTES),
        name="moba_prompt",
    )(qc, moba_t)


def _page_specs(layer, feat, group):
    return [
        pl.BlockSpec((None, None, feat, LANES),
                     functools.partial(lambda b, g, pt, j: (layer, pt[b, g * group + j], 0, 0), j=j))
        for j in range(group)
    ]


def _per_batch(shape):
    return pl.BlockSpec((None,) + tuple(shape), lambda b, g, pt: (b,) + (0,) * len(shape))


def _dsa_score_kernel(pt_ref, qi_ref, w_ref, new_ref, *rest, group):
    pages = rest[:group]
    past_ref, newsc_ref, ki_ref = rest[group:]
    g = pl.program_id(1)
    q = qi_ref[...].astype(BF16)
    w = w_ref[...] * (IDX_HEADS ** -0.5)
    nt = q.shape[0] // IDX_HEADS

    def head_sum(keys_t):
        rel = jnp.maximum(_dot(q, keys_t), 0.0) * w
        acc = rel[0:nt]
        for h in range(1, IDX_HEADS):
            acc = acc + rel[h * nt:(h + 1) * nt]
        return acc

    for j in range(group):
        ki_ref[:, j * LANES:(j + 1) * LANES] = pages[j][...].astype(BF16)
    past_ref[...] = head_sum(ki_ref[...])

    @pl.when(g == 0)
    def _():
        newsc_ref[...] = head_sum(new_ref[...].astype(BF16))


def _dsa_score(layer, page_table, idx_cache_t, qi_s, w_s, new_idx_t, *, group):
    b, n_pages = page_table.shape
    nt = qi_s.shape[1] // IDX_HEADS
    past = n_pages * LANES
    return pl.pallas_call(
        functools.partial(_dsa_score_kernel, group=group),
        grid_spec=pltpu.PrefetchScalarGridSpec(
            num_scalar_prefetch=1, grid=(b, n_pages // group),
            in_specs=[_per_batch(qi_s.shape[1:]), _per_batch(w_s.shape[1:]), _per_batch(new_idx_t.shape[1:])]
            + _page_specs(layer, IDX_DIM, group),
            out_specs=[pl.BlockSpec((None, nt, group * LANES), lambda bb, g, pt: (bb, 0, g)),
                       _per_batch((nt, LANES))],
            scratch_shapes=[pltpu.VMEM((IDX_DIM, group * LANES), BF16)]),
        out_shape=[jax.ShapeDtypeStruct((b, nt, past), F32), jax.ShapeDtypeStruct((b, nt, LANES), F32)],
        compiler_params=pltpu.CompilerParams(dimension_semantics=("arbitrary", "arbitrary"),
                                             vmem_limit_bytes=VMEM_LIMIT_BYTES),
        name="dsa_sample_score",
    )(page_table, qi_s, w_s, new_idx_t, *([idx_cache_t] * group))


def _dsa_select_kernel(past_ref, new_ref, mask_ref, *, ktop):
    rb, nt, past = past_ref.shape
    rows = rb * nt
    score = jnp.concatenate([past_ref[...].reshape(rows, past), new_ref[...].reshape(rows, LANES)], axis=1)
    idx = _lane_iota(score.shape)
    tok = _row_iota(score.shape) % nt
    valid = (idx < past) | (idx - past <= tok)
    score = jnp.where(score == 0.0, 0.0, score)
    score = jnp.where(valid, score, NEG_INF)
    sel = _topk_mask(score, ktop) & valid
    mask_ref[...] = jnp.where(sel, 1.0, 0.0).reshape(rb, nt, past + LANES)


def _dsa_select(sc_past, sc_new, *, rb):
    b, nt, past = sc_past.shape
    return pl.pallas_call(
        functools.partial(_dsa_select_kernel, ktop=min(DSA_TOPK, (past + nt) // 4)),
        grid=(b // rb,),
        in_specs=[pl.BlockSpec((rb, nt, past), lambda i: (i, 0, 0)), pl.BlockSpec((rb, nt, LANES), lambda i: (i, 0, 0))],
        out_specs=pl.BlockSpec((rb, nt, past + LANES), lambda i: (i, 0, 0)),
        out_shape=jax.ShapeDtypeStruct((b, nt, past + LANES), F32),
        compiler_params=pltpu.CompilerParams(dimension_semantics=("arbitrary",), vmem_limit_bytes=VMEM_LIMIT_BYTES),
        name="dsa_sample_select",
    )(sc_past, sc_new)


def _dsa_attn_kernel(pt_ref, q_ref, mpast_ref, mnew_ref, new_ref, *rest, group):
    pages = rest[:group]
    o_ref, k_ref, v_ref, m_ref, l_ref, acc_ref = rest[group:]
    g = pl.program_id(1)
    q = q_ref[...].astype(BF16)

    def masked(s, mask):
        allowed = jnp.concatenate([mask] * A_HEADS, axis=0) > 0.5
        return jnp.where(allowed, s, NEG_INF)

    @pl.when(g == 0)
    def _():
        m_ref[...] = jnp.full(m_ref.shape, NEG_INF, F32)
        l_ref[...] = jnp.zeros(l_ref.shape, F32)
        acc_ref[...] = jnp.zeros(acc_ref.shape, F32)
        new = new_ref[...].astype(BF16)
        _flash_update(masked(_dot(q, new[0:HEAD_DIM]), mnew_ref[...]), new[HEAD_DIM:2 * HEAD_DIM],
                      m_ref, l_ref, acc_ref)

    for j in range(group):
        page = pages[j][...].astype(BF16)
        k_ref[:, j * LANES:(j + 1) * LANES] = page[0:HEAD_DIM]
        v_ref[:, j * LANES:(j + 1) * LANES] = page[HEAD_DIM:2 * HEAD_DIM]
    _flash_update(masked(_dot(q, k_ref[...]), mpast_ref[...]), v_ref[...], m_ref, l_ref, acc_ref)

    @pl.when(g == pl.num_programs(1) - 1)
    def _():
        o_ref[...] = acc_ref[...] / l_ref[...]


def _dsa_attn(layer, page_table, kv_cache_t, qa_s, mask, new_kv_t, *, group):
    b, n_pages = page_table.shape
    rows = qa_s.shape[1]
    nt = rows // A_HEADS
    return pl.pallas_call(
        functools.partial(_dsa_attn_kernel, group=group),
        grid_spec=pltpu.PrefetchScalarGridSpec(
            num_scalar_prefetch=1, grid=(b, n_pages // group),
            in_specs=[_per_batch(qa_s.shape[1:]),
                      pl.BlockSpec((None, nt, group * LANES), lambda bb, g, pt: (bb, 0, g)),
                      pl.BlockSpec((None, nt, LANES), lambda bb, g, pt: (bb, 0, n_pages)),
                      _per_batch(new_kv_t.shape[1:])] + _page_specs(layer, 2 * HEAD_DIM, group),
            out_specs=_per_batch((rows, HEAD_DIM)),
            scratch_shapes=[pltpu.VMEM((HEAD_DIM, group * LANES), BF16), pltpu.VMEM((HEAD_DIM, group * LANES), BF16),
                            pltpu.VMEM((rows, 1), F32), pltpu.VMEM((rows, 1), F32), pltpu.VMEM((rows, HEAD_DIM), F32)]),
        out_shape=jax.ShapeDtypeStruct((b, rows, HEAD_DIM), F32),
        compiler_params=pltpu.CompilerParams(dimension_semantics=("arbitrary", "arbitrary"),
                                             vmem_limit_bytes=VMEM_LIMIT_BYTES),
        name="dsa_sample_attn",
    )(page_table, qa_s, mask, mask, new_kv_t, *([kv_cache_t] * group))


def _mla_sample_kernel(pt_ref, q_ref, new_ref, *rest, group):
    pages = rest[:group]
    o_ref, k_ref, m_ref, l_ref, acc_ref = rest[group:]
    g = pl.program_id(1)
    q = q_ref[...].astype(BF16)
    rows = q.shape[0]
    nt = rows // B_HEADS

    @pl.when(g == 0)
    def _():
        m_ref[...] = jnp.full(m_ref.shape, NEG_INF, F32)
        l_ref[...] = jnp.zeros(l_ref.shape, F32)
        acc_ref[...] = jnp.zeros(acc_ref.shape, F32)
        k_ref[LAT_DIM:256, :] = jnp.zeros((256 - LAT_DIM, group * LANES), BF16)
        k_ref[0:LAT_DIM, 0:LANES] = new_ref[...].astype(BF16)
        s = _dot(q, k_ref[:, 0:LANES])
        causal = _lane_iota((rows, LANES)) <= _row_iota((rows, LANES)) % nt
        _flash_update(jnp.where(causal, s, NEG_INF), k_ref[0:KV_LORA, 0:LANES], m_ref, l_ref, acc_ref, MLA_SCALE)

    for j in range(group):
        k_ref[0:LAT_DIM, j * LANES:(j + 1) * LANES] = pages[j][...].astype(BF16)
    _flash_update(_dot(q, k_ref[...]), k_ref[0:KV_LORA, :], m_ref, l_ref, acc_ref, MLA_SCALE)

    @pl.when(g == pl.num_programs(1) - 1)
    def _():
        o_ref[...] = acc_ref[...] / l_ref[...]


def _mla_sample(layer, page_table, mla_cache_t, qcat_s, new_mla_t, *, group):
    b, n_pages = page_table.shape
    rows = qcat_s.shape[1]
    return pl.pallas_call(
        functools.partial(_mla_sample_kernel, group=group),
        grid_spec=pltpu.PrefetchScalarGridSpec(
            num_scalar_prefetch=1, grid=(b, n_pages // group),
            in_specs=[_per_batch(qcat_s.shape[1:]), _per_batch(new_mla_t.shape[1:])]
            + _page_specs(layer, LAT_DIM, group),
            out_specs=_per_batch((rows, KV_LORA)),
            scratch_shapes=[pltpu.VMEM((256, group * LANES), BF16),
                            pltpu.VMEM((rows, 1), F32), pltpu.VMEM((rows, 1), F32), pltpu.VMEM((rows, KV_LORA), F32)]),
        out_shape=jax.ShapeDtypeStruct((b, rows, KV_LORA), F32),
        compiler_params=pltpu.CompilerParams(dimension_semantics=("arbitrary", "arbitrary"),
                                             vmem_limit_bytes=VMEM_LIMIT_BYTES),
        name="mla_sample",
    )(page_table, qcat_s, new_mla_t, *([mla_cache_t] * group))


def _moba_sample_kernel(pt_ref, q_ref, new_ref, *rest, group, n_past, n_sel):
    pages = rest[:group]
    o_ref, km_ref, ms_ref, ls_ref, ob_ref, k_ref, v_ref = rest[group:]
    g = pl.program_id(1)
    q = q_ref[...]
    qb = q.astype(BF16)
    rows = q.shape[0]
    nt = rows // C_HEADS
    half = 2 * HEAD_DIM
    lane = _lane_iota((rows, LANES))
    per_step = group // 2

    @pl.when(g == 0)
    def _():
        km_ref[...] = jnp.zeros(km_ref.shape, F32)
        ms_ref[...] = jnp.zeros(ms_ref.shape, F32)
        ls_ref[...] = jnp.zeros(ls_ref.shape, F32)

    for j in range(group):
        page = pages[j][...].astype(BF16)
        k_ref[:, j * LANES:(j + 1) * LANES] = page[0:half]
        v_ref[:, j * LANES:(j + 1) * LANES] = page[half:2 * half]
    s_all = _dot(qb, k_ref[...])
    km, ms, ls = km_ref[...], ms_ref[...], ls_ref[...]
    lane_k = _lane_iota((half, LANES))
    for jj in range(per_step):
        n = g * per_step + jj
        keys = slice(jj * MOBA_BLOCK, (jj + 1) * MOBA_BLOCK)
        kmean = (jnp.sum(pages[2 * jj][0:half, :], axis=1, keepdims=True)
                 + jnp.sum(pages[2 * jj + 1][0:half, :], axis=1, keepdims=True)) * (1.0 / MOBA_BLOCK)
        km = jnp.where(lane_k == n, kmean, km)
        s = s_all[:, keys]
        m = jnp.max(s, axis=-1, keepdims=True)
        p = jnp.exp(s - m)
        ms = jnp.where(lane == n, m, ms)
        ls = jnp.where(lane == n, jnp.sum(p, axis=-1, keepdims=True), ls)
        ob_ref[n] = _dot_nt(p.astype(BF16), v_ref[:, keys])
    km_ref[...], ms_ref[...], ls_ref[...] = km, ms, ls

    @pl.when(g == pl.num_programs(1) - 1)
    def _():
        gate = _dot(q, km_ref[...], precision=HIGHEST)
        sel = _topn_lanes(gate, lane < n_past, n_sel)
        new = new_ref[...].astype(BF16)
        s_own = jnp.where(lane <= _row_iota((rows, LANES)) % nt, _dot(qb, new[0:half]), NEG_INF)
        m_own = jnp.max(s_own, axis=-1, keepdims=True)
        m_sel = jnp.where(sel, ms_ref[...], NEG_INF)
        m_tot = jnp.maximum(jnp.max(m_sel, axis=-1, keepdims=True), m_own)
        p_own = jnp.exp(s_own - m_tot)
        wts = jnp.exp(m_sel - m_tot)
        l_tot = jnp.sum(wts * ls_ref[...], axis=-1, keepdims=True) + jnp.sum(p_own, axis=-1, keepdims=True)
        acc = _dot_nt(p_own.astype(BF16), new[half:2 * half])
        for n in range(n_past):
            acc = acc + wts[:, n:n + 1] * ob_ref[n]
        out = acc / l_tot
        upper = _row_iota((rows, LANES)) >= (C_HEADS // C_KV_HEADS) * nt
        o_ref[...] = jnp.where(upper, pltpu.roll(out, HEAD_DIM, 1), out)


def _moba_sample(layer, page_table, moba_cache_t, q2_s, new_moba_t, *, group):
    b, n_pages = page_table.shape
    rows = q2_s.shape[1]
    n_past = n_pages * LANES // MOBA_BLOCK
    return pl.pallas_call(
        functools.partial(_moba_sample_kernel, group=group, n_past=n_past, n_sel=min(MOBA_TOPK, n_past)),
        grid_spec=pltpu.PrefetchScalarGridSpec(
            num_scalar_prefetch=1, grid=(b, n_pages // group),
            in_specs=[_per_batch(q2_s.shape[1:]), _per_batch(new_moba_t.shape[1:])]
            + _page_specs(layer, 4 * HEAD_DIM, group),
            out_specs=_per_batch((rows, LANES)),
            scratch_shapes=[pltpu.VMEM((2 * HEAD_DIM, LANES), F32), pltpu.VMEM((rows, LANES), F32),
                            pltpu.VMEM((rows, LANES), F32), pltpu.VMEM((n_past, rows, LANES), F32),
                            pltpu.VMEM((2 * HEAD_DIM, group * LANES), BF16), pltpu.VMEM((2 * HEAD_DIM, group * LANES), BF16)]),
        out_shape=jax.ShapeDtypeStruct((b, rows, LANES), F32),
        compiler_params=pltpu.CompilerParams(dimension_semantics=("arbitrary", "arbitrary"),
                                             vmem_limit_bytes=VMEM_LIMIT_BYTES),
        name="moba_sample",
    )(page_table, q2_s, new_moba_t, *([moba_cache_t] * group))


def _rope_tables(pos):
    posf = pos.astype(F32)[:, None]

    def cos_sin(half):
        inv_freq = ROPE_THETA ** (-jnp.arange(half, dtype=F32) / half)
        ang = posf * inv_freq[None, :]
        return jnp.cos(ang), jnp.sin(ang)

    c32, s32 = cos_sin(HEAD_DIM // 2)
    c16, s16 = cos_sin(ROPE_DIM_B // 2)
    cos_a = jnp.tile(c32, (1, 4))
    sin_a = jnp.tile(jnp.concatenate([-s32, s32], axis=1), (1, 2))
    cos_q = jnp.tile(c16, (1, 8))
    sin_q = jnp.tile(jnp.concatenate([-s16, s16], axis=1), (1, 4))
    lane = jnp.arange(LANES)[None, :]
    in_kr = (lane >= 64) & (lane < 64 + ROPE_DIM_B)
    cos_m = jnp.where(in_kr, cos_q, 1.0)
    sin_m = jnp.where(in_kr, sin_q, 0.0)
    return jnp.concatenate([cos_a, sin_a, cos_m, sin_m, cos_q, sin_q], axis=1)


def _layer_weights(layer, norm_g, w_in, w_out, mla_g_q, mla_w_uq, mla_g_kv, mla_w_uk, mla_w_uv):
    w = w_in[layer]
    cols = [w[:, _OFF[n][0]:_OFF[n][1]] for n in _PERM]
    used = sum(c.shape[1] for c in cols)
    cols.append(jnp.zeros((w.shape[0], PROJ_WIDTH - used), w.dtype))
    wperm = jnp.concatenate(cols, axis=1).astype(BF16)
    uq = mla_w_uq[layer]
    wuq = jnp.concatenate([uq[:, :, :NOPE_DIM].reshape(Q_LORA, -1), uq[:, :, NOPE_DIM:].reshape(Q_LORA, -1)],
                          axis=1).astype(BF16)
    uk = jnp.transpose(mla_w_uk[layer], (1, 2, 0))
    uv = jnp.transpose(mla_w_uv[layer], (1, 0, 2))
    zk = jnp.zeros((NOPE_DIM, KV_LORA), F32)
    zv = jnp.zeros((KV_LORA, V_DIM_B), F32)
    wukp = jnp.stack([jnp.block([[uk[2 * p], zk], [zk, uk[2 * p + 1]]]) for p in range(B_HEADS // 2)]).astype(BF16)
    wuvp = jnp.stack([jnp.block([[uv[2 * p], zv], [zv, uv[2 * p + 1]]]) for p in range(B_HEADS // 2)]).astype(BF16)
    return dict(g=norm_g[layer][None, :], wperm=wperm, gq=mla_g_q[layer][None, :], wuq=wuq, wukp=wukp,
                gkv=mla_g_kv[layer][None, :], wuvp=wuvp, wout=w_out[layer].astype(BF16))


def _heads_first(a, b, nt, heads):
    d = a.shape[-1] // heads
    return a.reshape(b, nt, heads, d).transpose(0, 2, 1, 3).reshape(b, heads * nt, d)


def _tokens_first(a, b, nt, heads):
    d = a.shape[-1]
    return a.reshape(b, heads, nt, d).transpose(0, 2, 1, 3).reshape(b * nt, heads * d)


def _new_page(rows, b, nt):
    f = rows.shape[-1]
    return jnp.pad(rows.reshape(b, nt, f).transpose(0, 2, 1), ((0, 0), (0, 0), (0, LANES - nt)))


def kernel(x_prompt, x_sample, cache_dsa_kv, cache_dsa_idx, cache_mla, cache_moba_kv, page_table, norm_g, w_in, w_out, mla_g_q, mla_w_uq, mla_g_kv, mla_w_uk, mla_w_uv, final_g):
    depth = w_in.shape[0]
    bp, t, d = x_prompt.shape
    bs, nt, _ = x_sample.shape
    n_phys, page = cache_mla.shape[1], cache_mla.shape[2]
    n_pages = page_table.shape[1]
    past = n_pages * page
    assert page == LANES and t % MOBA_BLOCK == 0 and past % MOBA_BLOCK == 0 and nt == 8
    tm_p = 512
    tm_s = min(512, bs * nt)
    tq = 256
    group = min(128, n_pages)
    rb = min(8, bs)

    tab_p = _rope_tables(jnp.arange(t, dtype=jnp.int32))
    tab_s = jnp.tile(_rope_tables(past + jnp.arange(nt, dtype=jnp.int32)), (tm_s // nt, 1))
    fg = final_g[None, :]

    idx_t = jnp.transpose(cache_dsa_idx, (0, 1, 3, 2))
    kv_t = jnp.transpose(cache_dsa_kv, (0, 1, 3, 4, 2)).reshape(depth, n_phys, 2 * HEAD_DIM, page)
    mla_t = jnp.transpose(cache_mla, (0, 1, 3, 2))
    moba_t = jnp.transpose(cache_moba_kv, (0, 1, 3, 4, 5, 2)).reshape(depth, n_phys, 4 * HEAD_DIM, page)

    xp = x_prompt.reshape(bp * t, d)
    xs = x_sample.reshape(bs * nt, d)
    rows_p, rows_s = [], []
    for layer in range(depth):
        wl = _layer_weights(layer, norm_g, w_in, w_out, mla_g_q, mla_w_uq, mla_g_kv, mla_w_uk, mla_w_uv)
        final = layer == depth - 1

        qa, qi, qc, qcat, gate, aw, dkv_t, didx_t, lat_t, mob_t = _project(
            xp, tab_p, wl["g"], wl["wperm"], wl["gq"], wl["wuq"], wl["wukp"], wl["gkv"], tm=tm_p, batch_t=(bp, t))
        a_o = _dsa_prompt(qa, qi, aw, dkv_t, didx_t, tq=tq)
        b_lat = _mla_prompt(qcat, lat_t, tq=tq)
        c_o = _moba_prompt(qc, mob_t, tq=tq)
        xp = _outproj(a_o, b_lat, c_o, gate, xp, wl["wuvp"], wl["wout"], fg, tm=tm_p, final=final)
        rows_p.append((dkv_t, didx_t, lat_t, mob_t))

        qa, qi, qc, qcat, gate, aw, dkv, didx, lat, mob = _project(
            xs, tab_s, wl["g"], wl["wperm"], wl["gq"], wl["wuq"], wl["wukp"], wl["gkv"], tm=tm_s, batch_t=None)
        qa_s = _heads_first(qa.astype(F32), bs, nt, A_HEADS)
        qi_s = _heads_first(qi.astype(F32), bs, nt, IDX_HEADS)
        w_s = _heads_first(aw[:, ROPE_DIM_B:ROPE_DIM_B + IDX_HEADS], bs, nt, IDX_HEADS)
        qcat_s = qcat.astype(F32).reshape(B_HEADS, bs, nt, 256).transpose(1, 0, 2, 3).reshape(bs, B_HEADS * nt, 256)
        qc_h = _heads_first(qc, bs, nt, C_HEADS).reshape(bs, C_KV_HEADS, (C_HEADS // C_KV_HEADS) * nt, HEAD_DIM)
        zq = jnp.zeros_like(qc_h[:, 0])
        q2_s = jnp.concatenate([jnp.concatenate([qc_h[:, 0], zq], axis=-1),
                                jnp.concatenate([zq, qc_h[:, 1]], axis=-1)], axis=1)
        sc_past, sc_new = _dsa_score(layer, page_table, idx_t, qi_s, w_s, _new_page(didx, bs, nt), group=n_pages)
        mask = _dsa_select(sc_past, sc_new, rb=rb)
        a_o = _dsa_attn(layer, page_table, kv_t, qa_s, mask, _new_page(dkv, bs, nt), group=group)
        b_lat = _mla_sample(layer, page_table, mla_t, qcat_s, _new_page(lat, bs, nt), group=group)
        c_o = _moba_sample(layer, page_table, moba_t, q2_s, _new_page(mob, bs, nt), group=group)
        xs = _outproj(_tokens_first(a_o, bs, nt, A_HEADS), _tokens_first(b_lat, bs, nt, B_HEADS),
                      _tokens_first(c_o[:, :, :HEAD_DIM], bs, nt, C_HEADS), gate, xs,
                      wl["wuvp"], wl["wout"], fg, tm=tm_s, final=final)
        rows_s.append((dkv, didx, lat, mob))

    def prompt_rows(k, feat_shape):
        a = jnp.stack([r[k] for r in rows_p])
        a = a.reshape((depth, bp) + feat_shape + (t,))
        nd = a.ndim
        return jnp.transpose(a, (0, 1, nd - 1) + tuple(range(2, nd - 1)))

    def sample_rows(k, feat_shape):
        return jnp.stack([r[k] for r in rows_s]).reshape((depth, bs, nt) + feat_shape)

    shapes = ((2, HEAD_DIM), (IDX_DIM,), (LAT_DIM,), (2, C_KV_HEADS, HEAD_DIM))
    return (xp.reshape(bp, t, d), xs.reshape(bs, nt, d),
            *[prompt_rows(k, s) for k, s in enumerate(shapes)],
            *[sample_rows(k, s) for k, s in enumerate(shapes)])
```

```python
import functools

import jax
import jax.numpy as jnp
from jax import lax
from jax.experimental import pallas as pl
from jax.experimental.pallas import tpu as pltpu

F32 = jnp.float32
BF16 = jnp.bfloat16
NEG_INF = float("-inf")
HIGHEST = lax.Precision.HIGHEST

HEAD_DIM = 64
ROPE_THETA = 10000.0
NORM_EPS = 1e-6
A_HEADS = 4
IDX_HEADS = 4
IDX_DIM = 64
DSA_TOPK = 256
B_HEADS = 8
Q_LORA = 256
KV_LORA = 128
NOPE_DIM = 64
ROPE_DIM_B = 32
V_DIM_B = 64
MLA_SCALE = (NOPE_DIM + ROPE_DIM_B) ** -0.5
C_HEADS = 4
C_KV_HEADS = 2
MOBA_BLOCK = 256
MOBA_TOPK = 3
LAT_DIM = KV_LORA + ROPE_DIM_B

LANES = 128
VMEM_LIMIT_BYTES = 56 * 1024 * 1024

_SPLITS = (
    ("a_q", 256), ("a_k", 64), ("a_v", 64), ("a_qi", 256), ("a_ki", 64), ("a_w", 4), ("a_gate", 256),
    ("b_cq", 256), ("b_ckv", 128), ("b_kr", 32), ("b_gate", 512),
    ("c_q", 256), ("c_k", 128), ("c_v", 128), ("c_gate", 256),
)
_OFF = {}
_o = 0
for _n, _w in _SPLITS:
    _OFF[_n] = (_o, _o + _w)
    _o += _w
IN_WIDTH = _o
_PERM = ("a_q", "a_qi", "c_q", "c_k", "a_k", "a_ki", "a_gate", "b_gate", "c_gate", "b_cq", "b_ckv", "c_v",
         "a_v", "b_kr", "a_w")
PROJ_WIDTH = 2688
TAB_WIDTH = 6 * LANES


def _dot(a, b, precision=None):
    return jnp.dot(a, b, preferred_element_type=F32, precision=precision)


def _dot_nt(a, b):
    return lax.dot_general(a, b, (((1,), (1,)), ((), ())), preferred_element_type=F32)


def _lane_iota(shape):
    return lax.broadcasted_iota(jnp.int32, shape, len(shape) - 1)


def _row_iota(shape):
    return lax.broadcasted_iota(jnp.int32, shape, len(shape) - 2)


def _swap_half(x, half):
    n = x.shape[-1]
    lane = _lane_iota(x.shape)
    fwd = pltpu.roll(x, n - half, 1)
    bwd = pltpu.roll(x, half, 1)
    return jnp.where((lane & (2 * half - 1)) < half, fwd, bwd)


def _rms(x, g):
    return x * lax.rsqrt(jnp.mean(x * x, axis=-1, keepdims=True) + NORM_EPS) * g


def _topk_mask(score, k):
    rows, width = score.shape
    bits = lax.bitcast_convert_type(score, jnp.int32)
    u = jnp.where(bits < 0, bits ^ jnp.int32(0x7FFFFFFF), bits)
    kf = float(k)

    def count_ge(t):
        return jnp.sum(jnp.where(u >= t, 1.0, 0.0), axis=-1, keepdims=True)

    zero = jnp.zeros((rows, 1), jnp.int32)
    base = jnp.where(count_ge(zero) >= kf, zero, jnp.int32(-2 ** 31))

    def value_step(it, base):
        cand = base | jnp.left_shift(jnp.int32(1), 30 - it)
        return jnp.where(count_ge(cand) >= kf, cand, base)

    thr = lax.fori_loop(0, 31, value_step, base)
    gt = u > thr
    eq = u == thr
    need = kf - jnp.sum(jnp.where(gt, 1.0, 0.0), axis=-1, keepdims=True)
    idx = _lane_iota(score.shape)
    nbits = int(width).bit_length()

    def index_step(it, lim):
        cand = lim | jnp.left_shift(jnp.int32(1), nbits - 1 - it)
        below = jnp.sum(jnp.where(eq & (idx < cand), 1.0, 0.0), axis=-1, keepdims=True)
        return jnp.where(below < need, cand, lim)

    n_eq = jnp.sum(jnp.where(eq, 1.0, 0.0), axis=-1, keepdims=True)
    tied = jnp.max(jnp.where(n_eq != need, 1.0, 0.0)) > 0.5
    lim = lax.cond(tied,
                   lambda: lax.fori_loop(0, nbits, index_step, zero),
                   lambda: jnp.full((rows, 1), 2 ** nbits - 1, jnp.int32))
    return gt | (eq & (idx <= lim))


def _topn_lanes(gate, allowed, n_sel):
    lanef = _lane_iota(gate.shape).astype(F32)
    gm = jnp.where(allowed, gate, NEG_INF)
    sel = jnp.zeros(gate.shape, jnp.bool_)
    for _ in range(n_sel):
        m = jnp.max(gm, axis=-1, keepdims=True)
        first = jnp.min(jnp.where(gm == m, lanef, float(LANES)), axis=-1, keepdims=True)
        pick = lanef == first
        sel = sel | pick
        gm = jnp.where(pick, NEG_INF, gm)
    return sel & allowed


LOG2_E = 1.4426950408889634


def _softmax_pv(pieces, scale=1.0):
    m = None
    for s, _ in pieces:
        mj = jnp.max(s, axis=-1, keepdims=True)
        m = mj if m is None else jnp.maximum(m, mj)
    l = None
    o = None
    for s, v_t in pieces:
        p = jnp.exp2((s - m) * (scale * LOG2_E))
        lj = jnp.sum(p, axis=-1, keepdims=True)
        oj = _dot_nt(p.astype(BF16), v_t)
        l = lj if l is None else l + lj
        o = oj if o is None else o + oj
    return o / l


def _causal_bias(tq):
    return jnp.where(_lane_iota((tq, tq)) <= _row_iota((tq, tq)), 0.0, NEG_INF)


def _flash_update(s, v_t, m_ref, l_ref, acc_ref, scale=1.0):
    m_old = m_ref[...]
    m_new = jnp.maximum(m_old, jnp.max(s, axis=-1, keepdims=True))
    m_safe = jnp.where(m_new == NEG_INF, 0.0, m_new)
    alpha = jnp.exp2((m_old - m_safe) * (scale * LOG2_E))
    p = jnp.exp2((s - m_safe) * (scale * LOG2_E))
    l_ref[...] = alpha * l_ref[...] + jnp.sum(p, axis=-1, keepdims=True)
    acc_ref[...] = alpha * acc_ref[...] + _dot_nt(p.astype(BF16), v_t)
    m_ref[...] = m_new


def _project_kernel(x_ref, g_ref, w_ref, tab_ref, gq_ref, wuq_ref, wuk_ref, gkv_ref,
                    qa_ref, qi_ref, qc_ref, qcat_ref, gate_ref, aw_ref,
                    dkv_ref, didx_ref, mla_ref, moba_ref, *, rows_t):
    h = _rms(x_ref[...], g_ref[...]).astype(BF16)
    z = _dot(h, w_ref[...])
    tab = tab_ref[...]
    cos_a, sin_a, cos_m, sin_m, cos_q, sin_q = [tab[:, i * LANES:(i + 1) * LANES] for i in range(6)]
    lane = _lane_iota(cos_a.shape)

    def rope_a(c):
        zc = z[:, c * LANES:(c + 1) * LANES]
        return zc * cos_a + _swap_half(zc, 32) * sin_a

    r = [rope_a(c) for c in range(8)]
    for c in range(2):
        qa_ref[:, c * LANES:(c + 1) * LANES] = (r[c] * 0.125).astype(BF16)
        qi_ref[:, c * LANES:(c + 1) * LANES] = (r[2 + c] * 0.125).astype(BF16)
        qc_ref[:, c * LANES:(c + 1) * LANES] = r[4 + c] * 0.125
    gate_ref[...] = z[:, 1024:2048]

    cq = _rms(z[:, 2048:2304], gq_ref[...]).astype(BF16)
    qb = _dot(cq, wuq_ref[...])
    for p in range(B_HEADS // 2):
        ql = _dot(qb[:, p * LANES:(p + 1) * LANES].astype(BF16), wuk_ref[p])
        qcat_ref[2 * p, :, 0:LANES] = ql[:, 0:LANES].astype(BF16)
        qcat_ref[2 * p + 1, :, 0:LANES] = ql[:, LANES:2 * LANES].astype(BF16)
    for c in range(2):
        zc = qb[:, 512 + c * LANES:512 + (c + 1) * LANES]
        rq = zc * cos_q + _swap_half(zc, 16) * sin_q
        for j in range(4):
            sh = rq if j == 0 else pltpu.roll(rq, LANES - ROPE_DIM_B * j, 1)
            qcat_ref[4 * c + j, :, LANES:2 * LANES] = jnp.where(lane < ROPE_DIM_B, sh, 0.0).astype(BF16)

    ckv = _rms(z[:, 2304:2432], gkv_ref[...])
    misc = z[:, 2560:2688]
    misc = misc * cos_m + _swap_half(misc, 16) * sin_m
    rolled = pltpu.roll(misc, 64, 1)
    aw_ref[...] = rolled
    dkv = jnp.where(lane < 64, r[7], rolled)
    kidx = pltpu.roll(r[7], 64, 1)
    c_v = z[:, 2432:2560]
    if rows_t:
        dkv_ref[...] = dkv.T
        didx_ref[...] = kidx.T[0:IDX_DIM, :]
        mla_ref[0:KV_LORA, :] = ckv.T
        mla_ref[KV_LORA:LAT_DIM, :] = rolled.T[0:ROPE_DIM_B, :]
        moba_ref[0:LANES, :] = r[6].T
        moba_ref[LANES:2 * LANES, :] = c_v.T
    else:
        dkv_ref[...] = dkv
        didx_ref[...] = kidx[:, 0:IDX_DIM]
        mla_ref[:, 0:KV_LORA] = ckv
        mla_ref[:, KV_LORA:LAT_DIM] = rolled[:, 0:ROPE_DIM_B]
        moba_ref[:, 0:LANES] = r[6]
        moba_ref[:, LANES:2 * LANES] = c_v


def _project(x2d, tab, g, wperm, gq, wuq, wukp, gkv, *, tm, batch_t):
    n, d = x2d.shape
    steps = n // tm
    ntab = tab.shape[0] // tm
    row = lambda w: pl.BlockSpec((tm, w), lambda i: (i, 0))
    const2 = lambda a: pl.BlockSpec(a.shape, lambda i: (0, 0))
    in_specs = [
        row(d), const2(g), const2(wperm),
        pl.BlockSpec((tm, TAB_WIDTH), lambda i: (i % ntab, 0)),
        const2(gq), const2(wuq), pl.BlockSpec(wukp.shape, lambda i: (0, 0, 0)), const2(gkv),
    ]
    out_shape = [
        jax.ShapeDtypeStruct((n, 256), BF16), jax.ShapeDtypeStruct((n, 256), BF16),
        jax.ShapeDtypeStruct((n, 256), F32), jax.ShapeDtypeStruct((B_HEADS, n, 256), BF16),
        jax.ShapeDtypeStruct((n, 1024), F32), jax.ShapeDtypeStruct((n, LANES), F32),
    ]
    out_specs = [row(256), row(256), row(256), pl.BlockSpec((B_HEADS, tm, 256), lambda i: (0, i, 0)),
                 row(1024), row(LANES)]
    feats = (2 * HEAD_DIM, IDX_DIM, LAT_DIM, 4 * HEAD_DIM)
    if batch_t is not None:
        b, t = batch_t
        per = t // tm
        for f in feats:
            out_shape.append(jax.ShapeDtypeStruct((b, f, t), F32))
            out_specs.append(pl.BlockSpec((None, f, tm), lambda i: (i // per, 0, i % per)))
    else:
        for f in feats:
            out_shape.append(jax.ShapeDtypeStruct((n, f), F32))
            out_specs.append(row(f))
    return pl.pallas_call(
        functools.partial(_project_kernel, rows_t=batch_t is not None),
        grid=(steps,), in_specs=in_specs, out_specs=out_specs, out_shape=out_shape,
        compiler_params=pltpu.CompilerParams(dimension_semantics=("arbitrary",), vmem_limit_bytes=VMEM_LIMIT_BYTES),
        name="project",
    )(x2d, g, wperm, tab, gq, wuq, wukp, gkv)


def _outproj_kernel(ao_ref, bl_ref, co_ref, gate_ref, x_ref, wuv_ref, wout_ref, fg_ref, o_ref, mix_ref, *, final):
    gate = gate_ref[...]
    sg = gate * jax.nn.sigmoid(gate)
    mix_ref[:, 0:256] = (ao_ref[...] * sg[:, 0:256]).astype(BF16)
    bl = bl_ref[...].astype(BF16)
    for p in range(B_HEADS // 2):
        bo = _dot(bl[:, p * 256:(p + 1) * 256], wuv_ref[p])
        mix_ref[:, 256 + p * LANES:256 + (p + 1) * LANES] = (bo * sg[:, 256 + p * LANES:256 + (p + 1) * LANES]).astype(BF16)
    mix_ref[:, 768:1024] = (co_ref[...] * sg[:, 768:1024]).astype(BF16)
    xo = x_ref[...] + _dot(mix_ref[...], wout_ref[...])
    if final:
        xo = _rms(xo, fg_ref[...])
    o_ref[...] = xo


def _outproj(ao, bl, co, gate, x2d, wuvp, wout, fg, *, tm, final):
    n, d = x2d.shape
    row = lambda w: pl.BlockSpec((tm, w), lambda i: (i, 0))
    return pl.pallas_call(
        functools.partial(_outproj_kernel, final=final),
        grid=(n // tm,),
        in_specs=[row(256), row(1024), row(256), row(1024), row(d),
                  pl.BlockSpec(wuvp.shape, lambda i: (0, 0, 0)), pl.BlockSpec(wout.shape, lambda i: (0, 0)),
                  pl.BlockSpec(fg.shape, lambda i: (0, 0))],
        out_specs=row(d), out_shape=jax.ShapeDtypeStruct((n, d), F32),
        scratch_shapes=[pltpu.VMEM((tm, 1024), BF16)],
        compiler_params=pltpu.CompilerParams(dimension_semantics=("arbitrary",), vmem_limit_bytes=VMEM_LIMIT_BYTES),
        name="outproj",
    )(ao, bl, co, gate, x2d, wuvp, wout, fg)


def _for_bucket(i, t, tq, body):
    for j in range(t // tq):
        pl.when(i == j)(functools.partial(body, (j + 1) * tq))


def _dsa_prompt_kernel(qa_ref, qi_ref, aw_ref, kv_ref, ki_ref, o_ref, kih_ref, kh_ref, vh_ref, *, tq, t, ktop):
    i = pl.program_id(1)

    @pl.when(i == 0)
    def _():
        kv = kv_ref[...]
        k_t = kv[0:HEAD_DIM].astype(BF16)
        v_t = kv[HEAD_DIM:2 * HEAD_DIM].astype(BF16)
        ki_t = ki_ref[...].astype(BF16)
        zero = jnp.zeros((A_HEADS * HEAD_DIM, t), BF16)
        for h in range(A_HEADS):
            kih_ref[h] = zero
            kh_ref[h] = zero
            vh_ref[h] = zero
            kih_ref[h, h * HEAD_DIM:(h + 1) * HEAD_DIM, :] = ki_t
            kh_ref[h, h * HEAD_DIM:(h + 1) * HEAD_DIM, :] = k_t
            vh_ref[h, h * HEAD_DIM:(h + 1) * HEAD_DIM, :] = v_t

    def body(kl):
        row = i * tq + _row_iota((tq, kl))
        col = _lane_iota((tq, kl))
        causal = col <= row
        if kl <= ktop:
            sel = causal
        else:
            qi = qi_ref[...]
            aw = aw_ref[...]
            score = jnp.zeros((tq, kl), F32)
            for h in range(IDX_HEADS):
                rel = jnp.maximum(_dot(qi, kih_ref[h, :, 0:kl]), 0.0)
                score = score + rel * (aw[:, ROPE_DIM_B + h:ROPE_DIM_B + h + 1] * (IDX_HEADS ** -0.5))
            score = jnp.where(score == 0.0, 0.0, score)
            score = jnp.where(causal, score, NEG_INF)
            sel = _topk_mask(score, ktop) & causal
        bias = jnp.where(sel, 0.0, NEG_INF)
        qa = qa_ref[...]
        out = jnp.zeros((tq, A_HEADS * HEAD_DIM), F32)
        for h in range(A_HEADS):
            out = out + _softmax_pv([(_dot(qa, kh_ref[h, :, 0:kl]) + bias, vh_ref[h, :, 0:kl])])
        o_ref[...] = out

    _for_bucket(i, t, tq, body)


def _dsa_prompt(qa, qi, aw, dkv_t, didx_t, *, tq):
    b, _, t = dkv_t.shape
    nq = t // tq
    qrow = lambda w: pl.BlockSpec((tq, w), lambda bb, i: (bb * nq + i, 0))
    return pl.pallas_call(
        functools.partial(_dsa_prompt_kernel, tq=tq, t=t, ktop=min(DSA_TOPK, t // 4)),
        grid=(b, nq),
        in_specs=[qrow(256), qrow(256), qrow(LANES),
                  pl.BlockSpec((None, 2 * HEAD_DIM, t), lambda bb, i: (bb, 0, 0)),
                  pl.BlockSpec((None, IDX_DIM, t), lambda bb, i: (bb, 0, 0))],
        out_specs=qrow(256), out_shape=jax.ShapeDtypeStruct((b * t, 256), F32),
        scratch_shapes=[pltpu.VMEM((A_HEADS, 256, t), BF16)] * 3,
        compiler_params=pltpu.CompilerParams(dimension_semantics=("arbitrary", "arbitrary"),
                                             vmem_limit_bytes=VMEM_LIMIT_BYTES),
        name="dsa_prompt",
    )(qa, qi, aw, dkv_t, didx_t)


def _mla_prompt_kernel(qcat_ref, lat_ref, o_ref, kp_ref, *, tq, t):
    i = pl.program_id(1)

    @pl.when(i == 0)
    def _():
        kp_ref[0:LAT_DIM, :] = lat_ref[...].astype(BF16)
        kp_ref[LAT_DIM:256, :] = jnp.zeros((256 - LAT_DIM, t), BF16)

    def body(kl):
        diag = _causal_bias(tq)
        for h in range(B_HEADS):
            s = _dot(qcat_ref[h], kp_ref[:, 0:kl])
            pieces = [(s[:, kl - tq:kl] + diag, kp_ref[0:KV_LORA, kl - tq:kl])]
            if kl > tq:
                pieces.append((s[:, 0:kl - tq], kp_ref[0:KV_LORA, 0:kl - tq]))
            o_ref[:, h * KV_LORA:(h + 1) * KV_LORA] = _softmax_pv(pieces, scale=MLA_SCALE)

    _for_bucket(i, t, tq, body)


def _mla_prompt(qcat, mla_t, *, tq):
    b, _, t = mla_t.shape
    nq = t // tq
    return pl.pallas_call(
        functools.partial(_mla_prompt_kernel, tq=tq, t=t),
        grid=(b, nq),
        in_specs=[pl.BlockSpec((B_HEADS, tq, 256), lambda bb, i: (0, bb * nq + i, 0)),
                  pl.BlockSpec((None, LAT_DIM, t), lambda bb, i: (bb, 0, 0))],
        out_specs=pl.BlockSpec((tq, B_HEADS * KV_LORA), lambda bb, i: (bb * nq + i, 0)),
        out_shape=jax.ShapeDtypeStruct((b * t, B_HEADS * KV_LORA), F32),
        scratch_shapes=[pltpu.VMEM((256, t), BF16)],
        compiler_params=pltpu.CompilerParams(dimension_semantics=("arbitrary", "arbitrary"),
                                             vmem_limit_bytes=VMEM_LIMIT_BYTES),
        name="mla_prompt",
    )(qcat, mla_t)


GATE_LANES = LANES // C_HEADS


def _moba_prompt_kernel(qc_ref, rows_ref, o_ref, kh_ref, vh_ref, kmt_ref, *, tq, t, n_sel):
    i = pl.program_id(1)
    nb = t // MOBA_BLOCK

    @pl.when(i == 0)
    def _():
        rows = rows_ref[...]
        lane = _lane_iota((2 * HEAD_DIM, LANES))
        km = jnp.zeros((2 * HEAD_DIM, LANES), F32)
        for n in range(nb):
            col = jnp.sum(rows[0:2 * HEAD_DIM, n * MOBA_BLOCK:(n + 1) * MOBA_BLOCK], axis=1, keepdims=True)
            km = jnp.where(lane == n, col * (1.0 / MOBA_BLOCK), km)
        zero = jnp.zeros((C_HEADS * HEAD_DIM, t), BF16)
        for h in range(C_HEADS):
            g = h // (C_HEADS // C_KV_HEADS)
            kh_ref[h] = zero
            vh_ref[h] = zero
            kh_ref[h, h * HEAD_DIM:(h + 1) * HEAD_DIM, :] = rows[g * HEAD_DIM:(g + 1) * HEAD_DIM].astype(BF16)
            vh_ref[h, h * HEAD_DIM:(h + 1) * HEAD_DIM, :] = rows[(2 + g) * HEAD_DIM:(3 + g) * HEAD_DIM].astype(BF16)
            kmg = km[g * HEAD_DIM:(g + 1) * HEAD_DIM]
            kmt_ref[h * HEAD_DIM:(h + 1) * HEAD_DIM, :] = kmg if h == 0 else pltpu.roll(kmg, h * GATE_LANES, 1)

    def body(kl):
        q = qc_ref[...]
        qb = q.astype(BF16)
        lane = _lane_iota((tq, LANES))
        diag = _causal_bias(tq)
        n_past = kl // MOBA_BLOCK - 1
        gate = _dot(q, kmt_ref[...], precision=HIGHEST)
        out = jnp.zeros((tq, C_HEADS * HEAD_DIM), F32)
        for h in range(C_HEADS):
            past = (lane >= h * GATE_LANES) & (lane < h * GATE_LANES + n_past)
            blk_bias = jnp.where(_topn_lanes(gate, past, n_sel), 0.0, NEG_INF)
            s = _dot(qb, kh_ref[h, :, 0:kl])
            pieces = [(s[:, kl - tq:kl] + diag, vh_ref[h, :, kl - tq:kl])]
            for c in range(n_past):
                keys = slice(c * MOBA_BLOCK, (c + 1) * MOBA_BLOCK)
                pieces.append((s[:, keys] + blk_bias[:, h * GATE_LANES + c:h * GATE_LANES + c + 1], vh_ref[h, :, keys]))
            out = out + _softmax_pv(pieces)
        o_ref[...] = out

    _for_bucket(i, t, tq, body)


def _moba_prompt(qc, moba_t, *, tq):
    b, _, t = moba_t.shape
    nq = t // tq
    nb = t // MOBA_BLOCK
    assert nb <= GATE_LANES and tq == MOBA_BLOCK
    qrow = lambda w: pl.BlockSpec((tq, w), lambda bb, i: (bb * nq + i, 0))
    return pl.pallas_call(
        functools.partial(_moba_prompt_kernel, tq=tq, t=t, n_sel=min(MOBA_TOPK, nb - 1)),
        grid=(b, nq),
        in_specs=[qrow(256), pl.BlockSpec((None, 4 * HEAD_DIM, t), lambda bb, i: (bb, 0, 0))],
        out_specs=qrow(256), out_shape=jax.ShapeDtypeStruct((b * t, 256), F32),
        scratch_shapes=[pltpu.VMEM((C_HEADS, 256, t), BF16), pltpu.VMEM((C_HEADS, 256, t), BF16),
                        pltpu.VMEM((C_HEADS * HEAD_DIM, LANES), F32)],
        compiler_params=pltpu.CompilerParams(dimension_semantics=("arbitrary", "arbitrary"),
                                             vmem_limit_bytes=VMEM_LIMIT_BYTES),
        name="moba_prompt",
    )(qc, moba_t)


def _page_specs(layer, feat, group):
    return [
        pl.BlockSpec((None, None, feat, LANES),
                     functools.partial(lambda b, g, pt, j: (layer, pt[b, g * group + j], 0, 0), j=j))
        for j in range(group)
    ]


def _per_batch(shape):
    return pl.BlockSpec((None,) + tuple(shape), lambda b, g, pt: (b,) + (0,) * len(shape))


def _dsa_score_kernel(pt_ref, qi_ref, w_ref, new_ref, *rest, group):
    pages = rest[:group]
    past_ref, newsc_ref, ki_ref = rest[group:]
    g = pl.program_id(1)
    q = qi_ref[...].astype(BF16)
    w = w_ref[...] * (IDX_HEADS ** -0.5)
    nt = q.shape[0] // IDX_HEADS

    def head_sum(keys_t):
        rel = jnp.maximum(_dot(q, keys_t), 0.0) * w
        acc = rel[0:nt]
        for h in range(1, IDX_HEADS):
            acc = acc + rel[h * nt:(h + 1) * nt]
        return acc

    for j in range(group):
        ki_ref[:, j * LANES:(j + 1) * LANES] = pages[j][...].astype(BF16)
    past_ref[...] = head_sum(ki_ref[...])

    @pl.when(g == 0)
    def _():
        newsc_ref[...] = head_sum(new_ref[...].astype(BF16))


def _dsa_score(layer, page_table, idx_cache_t, qi_s, w_s, new_idx_t, *, group):
    b, n_pages = page_table.shape
    nt = qi_s.shape[1] // IDX_HEADS
    past = n_pages * LANES
    return pl.pallas_call(
        functools.partial(_dsa_score_kernel, group=group),
        grid_spec=pltpu.PrefetchScalarGridSpec(
            num_scalar_prefetch=1, grid=(b, n_pages // group),
            in_specs=[_per_batch(qi_s.shape[1:]), _per_batch(w_s.shape[1:]), _per_batch(new_idx_t.shape[1:])]
            + _page_specs(layer, IDX_DIM, group),
            out_specs=[pl.BlockSpec((None, nt, group * LANES), lambda bb, g, pt: (bb, 0, g)),
                       _per_batch((nt, LANES))],
            scratch_shapes=[pltpu.VMEM((IDX_DIM, group * LANES), BF16)]),
        out_shape=[jax.ShapeDtypeStruct((b, nt, past), F32), jax.ShapeDtypeStruct((b, nt, LANES), F32)],
        compiler_params=pltpu.CompilerParams(dimension_semantics=("arbitrary", "arbitrary"),
                                             vmem_limit_bytes=VMEM_LIMIT_BYTES),
        name="dsa_sample_score",
    )(page_table, qi_s, w_s, new_idx_t, *([idx_cache_t] * group))


def _dsa_select_kernel(past_ref, new_ref, mask_ref, *, ktop):
    rb, nt, past = past_ref.shape
    rows = rb * nt
    score = jnp.concatenate([past_ref[...].reshape(rows, past), new_ref[...].reshape(rows, LANES)], axis=1)
    idx = _lane_iota(score.shape)
    tok = _row_iota(score.shape) % nt
    valid = (idx < past) | (idx - past <= tok)
    score = jnp.where(score == 0.0, 0.0, score)
    score = jnp.where(valid, score, NEG_INF)
    sel = _topk_mask(score, ktop) & valid
    mask_ref[...] = jnp.where(sel, 1.0, 0.0).reshape(rb, nt, past + LANES)


def _dsa_select(sc_past, sc_new, *, rb):
    b, nt, past = sc_past.shape
    return pl.pallas_call(
        functools.partial(_dsa_select_kernel, ktop=min(DSA_TOPK, (past + nt) // 4)),
        grid=(b // rb,),
        in_specs=[pl.BlockSpec((rb, nt, past), lambda i: (i, 0, 0)), pl.BlockSpec((rb, nt, LANES), lambda i: (i, 0, 0))],
        out_specs=pl.BlockSpec((rb, nt, past + LANES), lambda i: (i, 0, 0)),
        out_shape=jax.ShapeDtypeStruct((b, nt, past + LANES), F32),
        compiler_params=pltpu.CompilerParams(dimension_semantics=("arbitrary",), vmem_limit_bytes=VMEM_LIMIT_BYTES),
        name="dsa_sample_select",
    )(sc_past, sc_new)


def _dsa_attn_kernel(pt_ref, q_ref, mpast_ref, mnew_ref, new_ref, *rest, group):
    pages = rest[:group]
    o_ref, k_ref, v_ref, m_ref, l_ref, acc_ref = rest[group:]
    g = pl.program_id(1)
    q = q_ref[...].astype(BF16)

    def masked(s, mask):
        allowed = jnp.concatenate([mask] * A_HEADS, axis=0) > 0.5
        return jnp.where(allowed, s, NEG_INF)

    @pl.when(g == 0)
    def _():
        m_ref[...] = jnp.full(m_ref.shape, NEG_INF, F32)
        l_ref[...] = jnp.zeros(l_ref.shape, F32)
        acc_ref[...] = jnp.zeros(acc_ref.shape, F32)
        new = new_ref[...].astype(BF16)
        _flash_update(masked(_dot(q, new[0:HEAD_DIM]), mnew_ref[...]), new[HEAD_DIM:2 * HEAD_DIM],
                      m_ref, l_ref, acc_ref)

    for j in range(group):
        page = pages[j][...].astype(BF16)
        k_ref[:, j * LANES:(j + 1) * LANES] = page[0:HEAD_DIM]
        v_ref[:, j * LANES:(j + 1) * LANES] = page[HEAD_DIM:2 * HEAD_DIM]
    _flash_update(masked(_dot(q, k_ref[...]), mpast_ref[...]), v_ref[...], m_ref, l_ref, acc_ref)

    @pl.when(g == pl.num_programs(1) - 1)
    def _():
        o_ref[...] = acc_ref[...] / l_ref[...]


def _dsa_attn(layer, page_table, kv_cache_t, qa_s, mask, new_kv_t, *, group):
    b, n_pages = page_table.shape
    rows = qa_s.shape[1]
    nt = rows // A_HEADS
    return pl.pallas_call(
        functools.partial(_dsa_attn_kernel, group=group),
        grid_spec=pltpu.PrefetchScalarGridSpec(
            num_scalar_prefetch=1, grid=(b, n_pages // group),
            in_specs=[_per_batch(qa_s.shape[1:]),
                      pl.BlockSpec((None, nt, group * LANES), lambda bb, g, pt: (bb, 0, g)),
                      pl.BlockSpec((None, nt, LANES), lambda bb, g, pt: (bb, 0, n_pages)),
                      _per_batch(new_kv_t.shape[1:])] + _page_specs(layer, 2 * HEAD_DIM, group),
            out_specs=_per_batch((rows, HEAD_DIM)),
            scratch_shapes=[pltpu.VMEM((HEAD_DIM, group * LANES), BF16), pltpu.VMEM((HEAD_DIM, group * LANES), BF16),
                            pltpu.VMEM((rows, 1), F32), pltpu.VMEM((rows, 1), F32), pltpu.VMEM((rows, HEAD_DIM), F32)]),
        out_shape=jax.ShapeDtypeStruct((b, rows, HEAD_DIM), F32),
        compiler_params=pltpu.CompilerParams(dimension_semantics=("arbitrary", "arbitrary"),
                                             vmem_limit_bytes=VMEM_LIMIT_BYTES),
        name="dsa_sample_attn",
    )(page_table, qa_s, mask, mask, new_kv_t, *([kv_cache_t] * group))


def _mla_sample_kernel(pt_ref, q_ref, new_ref, *rest, group):
    pages = rest[:group]
    o_ref, k_ref, m_ref, l_ref, acc_ref = rest[group:]
    g = pl.program_id(1)
    q = q_ref[...].astype(BF16)
    rows = q.shape[0]
    nt = rows // B_HEADS

    @pl.when(g == 0)
    def _():
        m_ref[...] = jnp.full(m_ref.shape, NEG_INF, F32)
        l_ref[...] = jnp.zeros(l_ref.shape, F32)
        acc_ref[...] = jnp.zeros(acc_ref.shape, F32)
        k_ref[LAT_DIM:256, :] = jnp.zeros((256 - LAT_DIM, group * LANES), BF16)
        k_ref[0:LAT_DIM, 0:LANES] = new_ref[...].astype(BF16)
        s = _dot(q, k_ref[:, 0:LANES])
        causal = _lane_iota((rows, LANES)) <= _row_iota((rows, LANES)) % nt
        _flash_update(jnp.where(causal, s, NEG_INF), k_ref[0:KV_LORA, 0:LANES], m_ref, l_ref, acc_ref, MLA_SCALE)

    for j in range(group):
        k_ref[0:LAT_DIM, j * LANES:(j + 1) * LANES] = pages[j][...].astype(BF16)
    _flash_update(_dot(q, k_ref[...]), k_ref[0:KV_LORA, :], m_ref, l_ref, acc_ref, MLA_SCALE)

    @pl.when(g == pl.num_programs(1) - 1)
    def _():
        o_ref[...] = acc_ref[...] / l_ref[...]


def _mla_sample(layer, page_table, mla_cache_t, qcat_s, new_mla_t, *, group):
    b, n_pages = page_table.shape
    rows = qcat_s.shape[1]
    return pl.pallas_call(
        functools.partial(_mla_sample_kernel, group=group),
        grid_spec=pltpu.PrefetchScalarGridSpec(
            num_scalar_prefetch=1, grid=(b, n_pages // group),
            in_specs=[_per_batch(qcat_s.shape[1:]), _per_batch(new_mla_t.shape[1:])]
            + _page_specs(layer, LAT_DIM, group),
            out_specs=_per_batch((rows, KV_LORA)),
            scratch_shapes=[pltpu.VMEM((256, group * LANES), BF16),
                            pltpu.VMEM((rows, 1), F32), pltpu.VMEM((rows, 1), F32), pltpu.VMEM((rows, KV_LORA), F32)]),
        out_shape=jax.ShapeDtypeStruct((b, rows, KV_LORA), F32),
        compiler_params=pltpu.CompilerParams(dimension_semantics=("arbitrary", "arbitrary"),
                                             vmem_limit_bytes=VMEM_LIMIT_BYTES),
        name="mla_sample",
    )(page_table, qcat_s, new_mla_t, *([mla_cache_t] * group))


def _moba_sample_kernel(pt_ref, q_ref, new_ref, *rest, group, n_past, n_sel):
    pages = rest[:group]
    o_ref, km_ref, ms_ref, ls_ref, ob_ref = rest[group:]
    g = pl.program_id(1)
    q = q_ref[...]
    qb = q.astype(BF16)
    rows = q.shape[0]
    nt = rows // C_HEADS
    half = 2 * HEAD_DIM
    lane = _lane_iota((rows, LANES))
    per_step = group // 2

    @pl.when(g == 0)
    def _():
        km_ref[...] = jnp.zeros(km_ref.shape, F32)
        ms_ref[...] = jnp.zeros(ms_ref.shape, F32)
        ls_ref[...] = jnp.zeros(ls_ref.shape, F32)

    km, ms, ls = km_ref[...], ms_ref[...], ls_ref[...]
    lane_k = _lane_iota((half, LANES))
    for jj in range(per_step):
        n = g * per_step + jj
        pa = pages[2 * jj][...]
        pb = pages[2 * jj + 1][...]
        k_t = jnp.concatenate([pa[0:half], pb[0:half]], axis=1)
        v_t = jnp.concatenate([pa[half:2 * half], pb[half:2 * half]], axis=1)
        kmean = jnp.sum(k_t, axis=1, keepdims=True) * (1.0 / MOBA_BLOCK)
        km = jnp.where(lane_k == n, kmean, km)
        s = _dot(qb, k_t.astype(BF16))
        m = jnp.max(s, axis=-1, keepdims=True)
        p = jnp.exp(s - m)
        ms = jnp.where(lane == n, m, ms)
        ls = jnp.where(lane == n, jnp.sum(p, axis=-1, keepdims=True), ls)
        ob_ref[n] = _dot_nt(p.astype(BF16), v_t.astype(BF16))
    km_ref[...], ms_ref[...], ls_ref[...] = km, ms, ls

    @pl.when(g == pl.num_programs(1) - 1)
    def _():
        gate = _dot(q, km_ref[...], precision=HIGHEST)
        sel = _topn_lanes(gate, lane < n_past, n_sel)
        new = new_ref[...].astype(BF16)
        s_own = jnp.where(lane <= _row_iota((rows, LANES)) % nt, _dot(qb, new[0:half]), NEG_INF)
        m_own = jnp.max(s_own, axis=-1, keepdims=True)
        m_sel = jnp.where(sel, ms_ref[...], NEG_INF)
        m_tot = jnp.maximum(jnp.max(m_sel, axis=-1, keepdims=True), m_own)
        p_own = jnp.exp(s_own - m_tot)
        wts = jnp.exp(m_sel - m_tot)
        l_tot = jnp.sum(wts * ls_ref[...], axis=-1, keepdims=True) + jnp.sum(p_own, axis=-1, keepdims=True)
        acc = _dot_nt(p_own.astype(BF16), new[half:2 * half])
        for n in range(n_past):
            acc = acc + wts[:, n:n + 1] * ob_ref[n]
        out = acc / l_tot
        upper = _row_iota((rows, LANES)) >= (C_HEADS // C_KV_HEADS) * nt
        o_ref[...] = jnp.where(upper, pltpu.roll(out, HEAD_DIM, 1), out)


def _moba_sample(layer, page_table, moba_cache_t, q2_s, new_moba_t, *, group):
    b, n_pages = page_table.shape
    rows = q2_s.shape[1]
    n_past = n_pages * LANES // MOBA_BLOCK
    return pl.pallas_call(
        functools.partial(_moba_sample_kernel, group=group, n_past=n_past, n_sel=min(MOBA_TOPK, n_past)),
        grid_spec=pltpu.PrefetchScalarGridSpec(
            num_scalar_prefetch=1, grid=(b, n_pages // group),
            in_specs=[_per_batch(q2_s.shape[1:]), _per_batch(new_moba_t.shape[1:])]
            + _page_specs(layer, 4 * HEAD_DIM, group),
            out_specs=_per_batch((rows, LANES)),
            scratch_shapes=[pltpu.VMEM((2 * HEAD_DIM, LANES), F32), pltpu.VMEM((rows, LANES), F32),
                            pltpu.VMEM((rows, LANES), F32), pltpu.VMEM((n_past, rows, LANES), F32)]),
        out_shape=jax.ShapeDtypeStruct((b, rows, LANES), F32),
        compiler_params=pltpu.CompilerParams(dimension_semantics=("arbitrary", "arbitrary"),
                                             vmem_limit_bytes=VMEM_LIMIT_BYTES),
        name="moba_sample",
    )(page_table, q2_s, new_moba_t, *([moba_cache_t] * group))


def _rope_tables(pos):
    posf = pos.astype(F32)[:, None]

    def cos_sin(half):
        inv_freq = ROPE_THETA ** (-jnp.arange(half, dtype=F32) / half)
        ang = posf * inv_freq[None, :]
        return jnp.cos(ang), jnp.sin(ang)

    c32, s32 = cos_sin(HEAD_DIM // 2)
    c16, s16 = cos_sin(ROPE_DIM_B // 2)
    cos_a = jnp.tile(c32, (1, 4))
    sin_a = jnp.tile(jnp.concatenate([-s32, s32], axis=1), (1, 2))
    cos_q = jnp.tile(c16, (1, 8))
    sin_q = jnp.tile(jnp.concatenate([-s16, s16], axis=1), (1, 4))
    lane = jnp.arange(LANES)[None, :]
    in_kr = (lane >= 64) & (lane < 64 + ROPE_DIM_B)
    cos_m = jnp.where(in_kr, cos_q, 1.0)
    sin_m = jnp.where(in_kr, sin_q, 0.0)
    return jnp.concatenate([cos_a, sin_a, cos_m, sin_m, cos_q, sin_q], axis=1)


def _layer_weights(layer, norm_g, w_in, w_out, mla_g_q, mla_w_uq, mla_g_kv, mla_w_uk, mla_w_uv):
    w = w_in[layer]
    cols = [w[:, _OFF[n][0]:_OFF[n][1]] for n in _PERM]
    used = sum(c.shape[1] for c in cols)
    cols.append(jnp.zeros((w.shape[0], PROJ_WIDTH - used), w.dtype))
    wperm = jnp.concatenate(cols, axis=1).astype(BF16)
    uq = mla_w_uq[layer]
    wuq = jnp.concatenate([uq[:, :, :NOPE_DIM].reshape(Q_LORA, -1), uq[:, :, NOPE_DIM:].reshape(Q_LORA, -1)],
                          axis=1).astype(BF16)
    uk = jnp.transpose(mla_w_uk[layer], (1, 2, 0))
    uv = jnp.transpose(mla_w_uv[layer], (1, 0, 2))
    zk = jnp.zeros((NOPE_DIM, KV_LORA), F32)
    zv = jnp.zeros((KV_LORA, V_DIM_B), F32)
    wukp = jnp.stack([jnp.block([[uk[2 * p], zk], [zk, uk[2 * p + 1]]]) for p in range(B_HEADS // 2)]).astype(BF16)
    wuvp = jnp.stack([jnp.block([[uv[2 * p], zv], [zv, uv[2 * p + 1]]]) for p in range(B_HEADS // 2)]).astype(BF16)
    return dict(g=norm_g[layer][None, :], wperm=wperm, gq=mla_g_q[layer][None, :], wuq=wuq, wukp=wukp,
                gkv=mla_g_kv[layer][None, :], wuvp=wuvp, wout=w_out[layer].astype(BF16))


def _heads_first(a, b, nt, heads):
    d = a.shape[-1] // heads
    return a.reshape(b, nt, heads, d).transpose(0, 2, 1, 3).reshape(b, heads * nt, d)


def _tokens_first(a, b, nt, heads):
    d = a.shape[-1]
    return a.reshape(b, heads, nt, d).transpose(0, 2, 1, 3).reshape(b * nt, heads * d)


def _new_page(rows, b, nt):
    f = rows.shape[-1]
    return jnp.pad(rows.reshape(b, nt, f).transpose(0, 2, 1), ((0, 0), (0, 0), (0, LANES - nt)))


def kernel(x_prompt, x_sample, cache_dsa_kv, cache_dsa_idx, cache_mla, cache_moba_kv, page_table, norm_g, w_in, w_out, mla_g_q, mla_w_uq, mla_g_kv, mla_w_uk, mla_w_uv, final_g):
    depth = w_in.shape[0]
    bp, t, d = x_prompt.shape
    bs, nt, _ = x_sample.shape
    n_phys, page = cache_mla.shape[1], cache_mla.shape[2]
    n_pages = page_table.shape[1]
    past = n_pages * page
    assert page == LANES and t % MOBA_BLOCK == 0 and past % MOBA_BLOCK == 0 and nt == 8
    tm_p = 512
    tm_s = min(512, bs * nt)
    tq = 256
    group = min(128, n_pages)
    rb = min(8, bs)

    tab_p = _rope_tables(jnp.arange(t, dtype=jnp.int32))
    tab_s = jnp.tile(_rope_tables(past + jnp.arange(nt, dtype=jnp.int32)), (tm_s // nt, 1))
    fg = final_g[None, :]

    idx_t = jnp.transpose(cache_dsa_idx, (0, 1, 3, 2))
    kv_t = jnp.transpose(cache_dsa_kv, (0, 1, 3, 4, 2)).reshape(depth, n_phys, 2 * HEAD_DIM, page)
    mla_t = jnp.transpose(cache_mla, (0, 1, 3, 2))
    moba_t = jnp.transpose(cache_moba_kv, (0, 1, 3, 4, 5, 2)).reshape(depth, n_phys, 4 * HEAD_DIM, page)

    xp = x_prompt.reshape(bp * t, d)
    xs = x_sample.reshape(bs * nt, d)
    rows_p, rows_s = [], []
    for layer in range(depth):
        wl = _layer_weights(layer, norm_g, w_in, w_out, mla_g_q, mla_w_uq, mla_g_kv, mla_w_uk, mla_w_uv)
        final = layer == depth - 1

        qa, qi, qc, qcat, gate, aw, dkv_t, didx_t, lat_t, mob_t = _project(
            xp, tab_p, wl["g"], wl["wperm"], wl["gq"], wl["wuq"], wl["wukp"], wl["gkv"], tm=tm_p, batch_t=(bp, t))
        a_o = _dsa_prompt(qa, qi, aw, dkv_t, didx_t, tq=tq)
        b_lat = _mla_prompt(qcat, lat_t, tq=tq)
        c_o = _moba_prompt(qc, mob_t, tq=tq)
        xp = _outproj(a_o, b_lat, c_o, gate, xp, wl["wuvp"], wl["wout"], fg, tm=tm_p, final=final)
        rows_p.append((dkv_t, didx_t, lat_t, mob_t))

        qa, qi, qc, qcat, gate, aw, dkv, didx, lat, mob = _project(
            xs, tab_s, wl["g"], wl["wperm"], wl["gq"], wl["wuq"], wl["wukp"], wl["gkv"], tm=tm_s, batch_t=None)
        qa_s = _heads_first(qa.astype(F32), bs, nt, A_HEADS)
        qi_s = _heads_first(qi.astype(F32), bs, nt, IDX_HEADS)
        w_s = _heads_first(aw[:, ROPE_DIM_B:ROPE_DIM_B + IDX_HEADS], bs, nt, IDX_HEADS)
        qcat_s = qcat.astype(F32).reshape(B_HEADS, bs, nt, 256).transpose(1, 0, 2, 3).reshape(bs, B_HEADS * nt, 256)
        qc_h = _heads_first(qc, bs, nt, C_HEADS).reshape(bs, C_KV_HEADS, (C_HEADS // C_KV_HEADS) * nt, HEAD_DIM)
        zq = jnp.zeros_like(qc_h[:, 0])
        q2_s = jnp.concatenate([jnp.concatenate([qc_h[:, 0], zq], axis=-1),
                                jnp.concatenate([zq, qc_h[:, 1]], axis=-1)], axis=1)
        sc_past, sc_new = _dsa_score(layer, page_table, idx_t, qi_s, w_s, _new_page(didx, bs, nt), group=n_pages)
        mask = _dsa_select(sc_past, sc_new, rb=rb)
        a_o = _dsa_attn(layer, page_table, kv_t, qa_s, mask, _new_page(dkv, bs, nt), group=group)
        b_lat = _mla_sample(layer, page_table, mla_t, qcat_s, _new_page(lat, bs, nt), group=group)
        c_o = _moba_sample(layer, page_table, moba_t, q2_s, _new_page(mob, bs, nt), group=min(64, n_pages))
        xs = _outproj(_tokens_first(a_o, bs, nt, A_HEADS), _tokens_first(b_lat, bs, nt, B_HEADS),
                      _tokens_first(c_o[:, :, :HEAD_DIM], bs, nt, C_HEADS), gate, xs,
                      wl["wuvp"], wl["wout"], fg, tm=tm_s, final=final)
        rows_s.append((dkv, didx, lat, mob))

    def prompt_rows(k, feat_shape):
        a = jnp.stack([r[k] for r in rows_p])
        a = a.reshape((depth, bp) + feat_shape + (t,))
        nd = a.ndim
        return jnp.transpose(a, (0, 1, nd - 1) + tuple(range(2, nd - 1)))

    def sample_rows(k, feat_shape):
        return jnp.stack([r[k] for r in rows_s]).reshape((depth, bs, nt) + feat_shape)

    shapes = ((2, HEAD_DIM), (IDX_DIM,), (LAT_DIM,), (2, C_KV_HEADS, HEAD_DIM))
    return (xp.reshape(bp, t, d), xs.reshape(bs, nt, d),
            *[prompt_rows(k, s) for k, s in enumerate(shapes)],
            *[sample_rows(k, s) for k, s in enumerate(shapes)])
```
